```python
import math
import jax, jax.numpy as jnp
from jax import lax
import numpy as np

D_MODEL = 4096
BATCH = 1
SEQ = 8192
DEPTH = 1
DEC_BATCH = 32
DEC_SEQ = 4
PAST_LEN = 8192
PAGE_SIZE = 128

HEAD_DIM = 128
MIX_WIDTH = D_MODEL
ATTN_WIDTH = MIX_WIDTH // 2
CONV_CH = MIX_WIDTH - ATTN_WIDTH
N_HEADS = ATTN_WIDTH // HEAD_DIM
DILATED_GROUPS = ((128, 1), (512, 4), (2048, 16))
MAX_WINDOW = max(w for w, _ in DILATED_GROUPS)
BLOCK = 128
ROPE_THETA = 500000.0
ROPE_DIM = HEAD_DIM // 4
CONV_K = 31
D_FF = ((8 * D_MODEL // 3 + 255) // 256) * 256
N_IN = 3 * ATTN_WIDTH + 2 * CONV_CH
RMS_EPS = 1e-6
LN_EPS = 1e-5

kernel_name = "hybrid_dilated_attn_conformer_conv_step"


def rmsnorm(x, g):
    xf = x.astype(jnp.float32)
    y = xf * lax.rsqrt(jnp.mean(xf * xf, axis=-1, keepdims=True) + RMS_EPS)
    return (y * g.astype(jnp.float32)).astype(x.dtype)


def layernorm(x, g, b):
    xf = x.astype(jnp.float32)
    mu = jnp.mean(xf, axis=-1, keepdims=True)
    var = jnp.mean(jnp.square(xf - mu), axis=-1, keepdims=True)
    y = (xf - mu) * lax.rsqrt(var + LN_EPS)
    return (y * g.astype(jnp.float32) + b.astype(jnp.float32)).astype(x.dtype)


def rope(x, pos):
    half = ROPE_DIM // 2
    inv = ROPE_THETA ** (-jnp.arange(half, dtype=jnp.float32) / half)
    ang = pos.astype(jnp.float32)[:, None] * inv[None, :]
    cos = jnp.cos(ang)[:, None, :]
    sin = jnp.sin(ang)[:, None, :]
    xr = x[..., :ROPE_DIM].astype(jnp.float32)
    x1, x2 = xr[..., :half], xr[..., half:]
    rot = jnp.concatenate([x1 * cos - x2 * sin, x2 * cos + x1 * sin], axis=-1).astype(x.dtype)
    return jnp.concatenate([rot, x[..., ROPE_DIM:]], axis=-1)


def project_in(h, w_in):
    B, S, _ = h.shape
    z = jnp.einsum('bsd,dn->bsn', h, w_in)
    A = ATTN_WIDTH
    q, k, v, a, g = jnp.split(z, [A, 2 * A, 3 * A, 3 * A + CONV_CH], axis=-1)
    shp = (B, S, N_HEADS, HEAD_DIM)
    u = a * jax.nn.sigmoid(g)
    return q.reshape(shp), k.reshape(shp), v.reshape(shp), u


def dilated_group_prompt(q, k, v, dil, span):
    B, Sp, H, DH = q.shape
    L = Sp // dil
    nb = L // BLOCK

    def split(x):
        return x.reshape(B, L, dil, H, DH).transpose(0, 2, 1, 3, 4).reshape(B, dil, nb, BLOCK, H, DH)

    def with_prev(x):
        prev = jnp.pad(x[:, :, :-1], ((0, 0), (0, 0), (1, 0), (0, 0), (0, 0), (0, 0)))
        return jnp.concatenate([prev, x], axis=3)

    qb = split(q)
    kc = with_prev(split(k))
    vc = with_prev(split(v))
    s = jnp.einsum('brnqhd,brnkhd->brnhqk', qb, kc).astype(jnp.float32) * (HEAD_DIM ** -0.5)
    qi = jnp.arange(BLOCK)[:, None]
    ki = jnp.arange(2 * BLOCK)[None, :]
    dist = BLOCK + qi - ki
    band = (dist >= 0) & (dist <= span)
    not_first = (jnp.arange(nb) > 0)[:, None, None] | (ki >= BLOCK)[None]
    valid = band[None] & not_first
    s = jnp.where(valid[None, None, :, None], s, -jnp.inf)
    m = jnp.max(s, axis=-1)
    p = jnp.exp(s - m[..., None])
    den = jnp.sum(p, axis=-1)
    num = jnp.einsum('brnhqk,brnkhd->brnqhd', p, vc.astype(jnp.float32))

    def unsplit(x):
        tail = x.shape[5:]
        x = x.reshape((B, dil, L, H) + tail)
        x = jnp.swapaxes(x, 1, 2)
        return x.reshape((B, Sp, H) + tail)

    m = jnp.swapaxes(m, 3, 4)[..., None]
    den = jnp.swapaxes(den, 3, 4)[..., None]
    return unsplit(num), unsplit(m)[..., 0], unsplit(den)[..., 0]


def dilated_attn_prompt(q, k, v):
    B, S, H, DH = q.shape
    unit = BLOCK
    for _, d in DILATED_GROUPS:
        unit = unit * d // math.gcd(unit, d * BLOCK) * BLOCK // BLOCK if False else math.lcm(unit, d * BLOCK)
    Sp = -(-S // unit) * unit
    pad = ((0, 0), (0, Sp - S), (0, 0), (0, 0))
    qp, kp, vp = jnp.pad(q, pad), jnp.pad(k, pad), jnp.pad(v, pad)
    parts = [dilated_group_prompt(qp, kp, vp, d, w // d) for w, d in DILATED_GROUPS]
    M = parts[0][1]
    for _, m, _ in parts[1:]:
        M = jnp.maximum(M, m)
    num = 0.0
    den = 0.0
    for n_g, m_g, d_g in parts:
        sc = jnp.exp(m_g - M)
        num = num + n_g * sc[..., None]
        den = den + d_g * sc
    out = num / den[..., None]
    return out[:, :S].astype(q.dtype)


def dilated_attn_sample(q, k_ctx, v_ctx, buf):
    T = q.shape[1]
    dists = jnp.concatenate([jnp.arange(w // d + 1) * d for w, d in DILATED_GROUPS])
    idx = buf + jnp.arange(T)[:, None] - dists[None, :]
    valid = idx >= 0
    idx = jnp.maximum(idx, 0)
    kg = k_ctx[:, idx]
    vg = v_ctx[:, idx]
    s = jnp.einsum('bthd,btkhd->bthk', q, kg).astype(jnp.float32) * (HEAD_DIM ** -0.5)
    s = jnp.where(valid[None, :, None, :], s, -jnp.inf)
    p = jax.nn.softmax(s, axis=-1)
    out = jnp.einsum('bthk,btkhd->bthd', p, vg.astype(jnp.float32))
    return out.astype(q.dtype)


def conformer_conv_tail(u_ctx, conv_w, conv_b, ln_g, ln_b):
    C = u_ctx.shape[-1]
    y = lax.conv_general_dilated(u_ctx, conv_w[:, None, :], window_strides=(1,), padding='VALID',
                                 dimension_numbers=('NWC', 'WIO', 'NWC'), feature_group_count=C)
    y = y + conv_b
    return jax.nn.silu(layernorm(y, ln_g, ln_b))


def finish_layer(x, attn, conv, w_out, norm2_g, w_gate, w_up, w_down):
    B, S = x.shape[:2]
    mix = jnp.concatenate([attn.reshape(B, S, ATTN_WIDTH), conv], axis=-1)
    x = x + jnp.einsum('bsm,md->bsd', mix, w_out)
    h = rmsnorm(x, norm2_g)
    f = jax.nn.silu(jnp.einsum('bsd,df->bsf', h, w_gate)) * jnp.einsum('bsd,df->bsf', h, w_up)
    return x + jnp.einsum('bsf,fd->bsd', f, w_down)


def setup_inputs(seed: int = 0) -> dict:
    key = jax.random.key(seed)
    ks = jax.random.split(key, 20)
    f32 = jnp.float32
    buf = min(MAX_WINDOW, PAST_LEN)
    nrm = lambda k, shp, sc: jax.random.normal(k, shp, f32) * sc
    return {
        "x_prompt": nrm(ks[0], (BATCH, SEQ, D_MODEL), 1.0),
        "x_sample": nrm(ks[1], (DEC_BATCH, DEC_SEQ, D_MODEL), 1.0),
        "cache_k": nrm(ks[2], (DEPTH, DEC_BATCH, buf, N_HEADS, HEAD_DIM), 1.0),
        "cache_v": nrm(ks[3], (DEPTH, DEC_BATCH, buf, N_HEADS, HEAD_DIM), 1.0),
        "state_conv": nrm(ks[4], (DEPTH, DEC_BATCH, CONV_K - 1, CONV_CH), 0.5),
        "norm1_g": 1.0 + nrm(ks[5], (DEPTH, D_MODEL), 0.02),
        "w_in": nrm(ks[6], (DEPTH, D_MODEL, N_IN), D_MODEL ** -0.5),
        "conv_w": nrm(ks[7], (DEPTH, CONV_K, CONV_CH), CONV_K ** -0.5),
        "conv_b": nrm(ks[8], (DEPTH, CONV_CH), 0.02),
        "conv_ln_g": 1.0 + nrm(ks[9], (DEPTH, CONV_CH), 0.02),
        "conv_ln_b": nrm(ks[10], (DEPTH, CONV_CH), 0.02),
        "w_out": nrm(ks[11], (DEPTH, MIX_WIDTH, D_MODEL), MIX_WIDTH ** -0.5),
        "norm2_g": 1.0 + nrm(ks[12], (DEPTH, D_MODEL), 0.02),
        "w_gate": nrm(ks[13], (DEPTH, D_MODEL, D_FF), D_MODEL ** -0.5),
        "w_up": nrm(ks[14], (DEPTH, D_MODEL, D_FF), D_MODEL ** -0.5),
        "w_down": nrm(ks[15], (DEPTH, D_FF, D_MODEL), D_FF ** -0.5),
        "final_g": 1.0 + nrm(ks[16], (D_MODEL,), 0.02),
    }


def reference(x_prompt, x_sample, cache_k, cache_v, state_conv, norm1_g, w_in, conv_w, conv_b,
              conv_ln_g, conv_ln_b, w_out, norm2_g, w_gate, w_up, w_down, final_g):
    S = x_prompt.shape[1]
    T = x_sample.shape[1]
    pos_p = jnp.arange(S, dtype=jnp.int32)
    pos_s = PAST_LEN + jnp.arange(T, dtype=jnp.int32)
    xp, xs = x_prompt, x_sample
    nkp, nvp, ncp, nks, nvs, ncs = [], [], [], [], [], []
    for l in range(DEPTH):
        h = rmsnorm(xp, norm1_g[l])
        q, k, v, u = project_in(h, w_in[l])
        q, k = rope(q, pos_p), rope(k, pos_p)
        attn = dilated_attn_prompt(q, k, v)
        u_ctx = jnp.pad(u, ((0, 0), (CONV_K - 1, 0), (0, 0)))
        conv = conformer_conv_tail(u_ctx, conv_w[l], conv_b[l], conv_ln_g[l], conv_ln_b[l])
        xp = finish_layer(xp, attn, conv, w_out[l], norm2_g[l], w_gate[l], w_up[l], w_down[l])
        keep_p = min(MAX_WINDOW, S)
        nkp.append(k[:, S - keep_p:])
        nvp.append(v[:, S - keep_p:])
        ncp.append(u_ctx[:, -(CONV_K - 1):])
        h = rmsnorm(xs, norm1_g[l])
        q, k, v, u = project_in(h, w_in[l])
        q, k = rope(q, pos_s), rope(k, pos_s)
        buf = cache_k.shape[2]
        k_ctx = jnp.concatenate([cache_k[l], k], axis=1)
        v_ctx = jnp.concatenate([cache_v[l], v], axis=1)
        attn = dilated_attn_sample(q, k_ctx, v_ctx, buf)
        u_ctx = jnp.concatenate([state_conv[l], u], axis=1)
        conv = conformer_conv_tail(u_ctx, conv_w[l], conv_b[l], conv_ln_g[l], conv_ln_b[l])
        xs = finish_layer(xs, attn, conv, w_out[l], norm2_g[l], w_gate[l], w_up[l], w_down[l])
        keep_s = min(MAX_WINDOW, buf + T)
        nks.append(k_ctx[:, buf + T - keep_s:])
        nvs.append(v_ctx[:, buf + T - keep_s:])
        ncs.append(u_ctx[:, -(CONV_K - 1):])
    y_prompt = rmsnorm(xp, final_g)
    y_sample = rmsnorm(xs, final_g)
    return (y_prompt, y_sample, jnp.stack(nkp), jnp.stack(nvp), jnp.stack(ncp),
            jnp.stack(nks), jnp.stack(nvs), jnp.stack(ncs))
```

```python
import functools
import math

import jax
import jax.numpy as jnp
from jax import lax
from jax.experimental import pallas as pl
from jax.experimental.pallas import tpu as pltpu

F32 = jnp.float32
BF16 = jnp.bfloat16

HEAD_DIM = 128
ROPE_DIM = HEAD_DIM // 4
ROPE_THETA = 500000.0
CONV_K = 31
RMS_EPS = 1e-6
LN_EPS = 1e-5
PAST_LEN = 8192
DILATIONS = (1, 4, 16)
SPAN = 128
MAX_WINDOW = 2048
ATTN_TILE = MAX_WINDOW
CONV_HALO = 32

LANES = 128
VMEM_LIMIT_BYTES = 56 * 1024 * 1024


def _params(*sem):
    return pltpu.CompilerParams(dimension_semantics=sem, vmem_limit_bytes=VMEM_LIMIT_BYTES)


def _tile(n, pref, mult=8):
    if n <= pref:
        return n
    for t in range(pref - pref % mult, 0, -mult):
        if n % t == 0:
            return t
    raise ValueError(f"no tile for {n}")


def _rms_to_bf16(x, g):
    ms = jnp.mean(x * x, axis=-1, keepdims=True)
    return (x * lax.rsqrt(ms + RMS_EPS) * g).astype(BF16)


def _dot(a, b):
    return jnp.dot(a, b, preferred_element_type=F32)


def _dot_nt(a, b):
    return lax.dot_general(a, b, (((1,), (1,)), ((), ())), preferred_element_type=F32)


def _qkv_kernel(x_ref, g_ref, w_ref, c_ref, s1_ref, s2_ref, o_ref, h_ref, *, n_rope_tiles):
    n = pl.program_id(1)

    @pl.when(n == 0)
    def _():
        h_ref[...] = _rms_to_bf16(x_ref[...], g_ref[...])

    z = _dot(h_ref[...], w_ref[...])

    @pl.when(n < n_rope_tiles)
    def _():
        c, s1, s2 = c_ref[...], s1_ref[...], s2_ref[...]
        for j in range(z.shape[1] // HEAD_DIM):
            zj = z[:, j * HEAD_DIM:(j + 1) * HEAD_DIM]
            lo = pltpu.roll(zj, ROPE_DIM // 2, 1)
            hi = pltpu.roll(zj, HEAD_DIM - ROPE_DIM // 2, 1)
            o_ref[:, j * HEAD_DIM:(j + 1) * HEAD_DIM] = zj * c + lo * s1 + hi * s2

    @pl.when(n >= n_rope_tiles)
    def _():
        o_ref[...] = z


def _glu_kernel(x_ref, g_ref, wa_ref, wg_ref, u_ref, h_ref):
    @pl.when(pl.program_id(1) == 0)
    def _():
        h_ref[...] = _rms_to_bf16(x_ref[...], g_ref[...])

    h = h_ref[...]
    u_ref[...] = _dot(h, wa_ref[...]) * jax.nn.sigmoid(_dot(h, wg_ref[...]))


def _rope_tables(pos):
    half = ROPE_DIM // 2
    inv = ROPE_THETA ** (-jnp.arange(half, dtype=F32) / half)
    ang = pos.astype(F32)[:, None] * inv[None, :]
    cos, sin = jnp.cos(ang), jnp.sin(ang)
    m = pos.shape[0]
    one = jnp.ones((m, HEAD_DIM - ROPE_DIM), F32)
    zero = jnp.zeros((m, HEAD_DIM - ROPE_DIM), F32)
    zh = jnp.zeros((m, half), F32)
    c = jnp.concatenate([cos, cos, one], axis=1)
    s1 = jnp.concatenate([zh, sin, zero], axis=1)
    s2 = jnp.concatenate([-sin, zh, zero], axis=1)
    return c, s1, s2


def _project_in(x, g1, w_in, pos, attn_w, conv_c):
    m, d = x.shape
    tm = _tile(m, 512)
    tn = _tile(attn_w, 512, LANES)
    c, s1, s2 = _rope_tables(pos)
    tab = pl.BlockSpec((tm, HEAD_DIM), lambda i, n: (i, 0))
    qkv = pl.pallas_call(
        functools.partial(_qkv_kernel, n_rope_tiles=2 * attn_w // tn),
        grid=(m // tm, 3 * attn_w // tn),
        in_specs=[pl.BlockSpec((tm, d), lambda i, n: (i, 0)),
                  pl.BlockSpec((1, d), lambda i, n: (0, 0)),
                  pl.BlockSpec((d, tn), lambda i, n: (0, n)),
                  tab, tab, tab],
        out_specs=pl.BlockSpec((tm, tn), lambda i, n: (i, n)),
        out_shape=jax.ShapeDtypeStruct((m, 3 * attn_w), F32),
        scratch_shapes=[pltpu.VMEM((tm, d), BF16)],
        compiler_params=_params("parallel", "arbitrary"),
        name="qkv_proj",
    )(x, g1, w_in, c, s1, s2)

    tc = _tile(conv_c, 512, LANES)
    a0 = 3 * attn_w // tc
    g0 = (3 * attn_w + conv_c) // tc
    u = pl.pallas_call(
        _glu_kernel,
        grid=(m // tm, conv_c // tc),
        in_specs=[pl.BlockSpec((tm, d), lambda i, n: (i, 0)),
                  pl.BlockSpec((1, d), lambda i, n: (0, 0)),
                  pl.BlockSpec((d, tc), lambda i, n: (0, a0 + n)),
                  pl.BlockSpec((d, tc), lambda i, n: (0, g0 + n))],
        out_specs=pl.BlockSpec((tm, tc), lambda i, n: (i, n)),
        out_shape=jax.ShapeDtypeStruct((m, conv_c), F32),
        scratch_shapes=[pltpu.VMEM((tm, d), BF16)],
        compiler_params=_params("parallel", "arbitrary"),
        name="glu_proj",
    )(x, g1, w_in, w_in)
    return qkv, u


def _rows(start, size, stride):
    return pl.ds(start, size) if stride == 1 else pl.ds(start, size, stride=stride)


def _attn_prompt_kernel(q_ref, kp_ref, kc_ref, vp_ref, vc_ref, o_ref, num_scr, m_scr, l_scr):
    tile = pl.program_id(1)
    t = q_ref.shape[0]
    scale = HEAD_DIM ** -0.5
    qi = lax.broadcasted_iota(jnp.int32, (SPAN, 2 * SPAN), 0)
    kj = lax.broadcasted_iota(jnp.int32, (SPAN, 2 * SPAN), 1)
    band = (kj >= qi) & (kj <= qi + SPAN)
    first_key = jnp.where(tile > 0, 0, SPAN)
    band_first = band & (kj >= first_key)

    def attend(gi, d, q_start, q, k, v, mask):
        s = _dot_nt(q.astype(BF16), k.astype(BF16)) * scale
        s = jnp.where(mask, s, -jnp.inf)
        m = jnp.max(s, axis=-1, keepdims=True)
        p = jnp.exp(s - m)
        l = jnp.sum(p, axis=-1, keepdims=True)
        num = _dot(p.astype(BF16), v.astype(BF16))
        rows = _rows(q_start, SPAN, d)
        num_scr[gi, rows, :] = num
        m_scr[gi, rows, :] = jnp.broadcast_to(m, (SPAN, HEAD_DIM))
        l_scr[gi, rows, :] = jnp.broadcast_to(l, (SPAN, HEAD_DIM))

    for gi, d in enumerate(DILATIONS):
        n_blocks = t // (d * SPAN)

        def class_body(r, carry, gi=gi, d=d, n_blocks=n_blocks):
            q = q_ref[_rows(r, SPAN, d), :]
            k = jnp.concatenate([kp_ref[_rows(t - d * SPAN + r, SPAN, d), :],
                                 kc_ref[_rows(r, SPAN, d), :]], axis=0)
            v = jnp.concatenate([vp_ref[_rows(t - d * SPAN + r, SPAN, d), :],
                                 vc_ref[_rows(r, SPAN, d), :]], axis=0)
            attend(gi, d, r, q, k, v, band_first)

            def block_body(b, c):
                q_start = r + d * SPAN * b
                if d == 1:
                    q_start = pl.multiple_of(q_start, SPAN)
                k_rows = _rows(q_start - d * SPAN, 2 * SPAN, d)
                attend(gi, d, q_start, q_ref[_rows(q_start, SPAN, d), :],
                       kc_ref[k_rows, :], vc_ref[k_rows, :], band)
                return c

            if n_blocks > 1:
                lax.fori_loop(1, n_blocks, block_body, 0)
            return carry

        if d == 1:
            class_body(0, 0)
        else:
            lax.fori_loop(0, d, class_body, 0)

    def combine(c, carry):
        rows = pl.ds(pl.multiple_of(c * SPAN, SPAN), SPAN)
        ms = [m_scr[g, rows, :] for g in range(len(DILATIONS))]
        m_all = functools.reduce(jnp.maximum, ms)
        num = jnp.zeros((SPAN, HEAD_DIM), F32)
        den = jnp.zeros((SPAN, HEAD_DIM), F32)
        for g in range(len(DILATIONS)):
            sc = jnp.exp(ms[g] - m_all)
            num = num + num_scr[g, rows, :] * sc
            den = den + l_scr[g, rows, :] * sc
        o_ref[rows, :] = (num / den).astype(o_ref.dtype)
        return carry

    lax.fori_loop(0, t // SPAN, combine, 0)


def _attn_prompt(qkv, n_heads):
    s = qkv.shape[0]
    t = ATTN_TILE
    assert s % t == 0, "prompt length must be a multiple of the dilation tile"
    cur = lambda off: pl.BlockSpec((t, HEAD_DIM), lambda h, i: (i, off + h))
    prev = lambda off: pl.BlockSpec((t, HEAD_DIM), lambda h, i: (jnp.maximum(i - 1, 0), off + h))
    return pl.pallas_call(
        _attn_prompt_kernel,
        grid=(n_heads, s // t),
        in_specs=[cur(0), prev(n_heads), cur(n_heads), prev(2 * n_heads), cur(2 * n_heads)],
        out_specs=pl.BlockSpec((t, HEAD_DIM), lambda h, i: (i, h)),
        out_shape=jax.ShapeDtypeStruct((s, n_heads * HEAD_DIM), BF16),
        scratch_shapes=[pltpu.VMEM((len(DILATIONS), t, HEAD_DIM), F32)] * 3,
        compiler_params=_params("parallel", "arbitrary"),
        name="attn_prompt",
    )(qkv, qkv, qkv, qkv, qkv)


def _attn_sample_kernel(qkv_ref, kt_ref, vt_ref, kf_ref, vf_ref, o_ref, kn_scr, vn_scr, *, n_heads):
    n_new = qkv_ref.shape[1]
    tail = kt_ref.shape[1]
    d_far = DILATIONS[-1]
    kn_scr[0:tail] = kt_ref[0]
    vn_scr[0:tail] = vt_ref[0]
    kn_scr[tail:tail + n_new] = qkv_ref[0, :, n_heads:2 * n_heads, :]
    vn_scr[tail:tail + n_new] = qkv_ref[0, :, 2 * n_heads:3 * n_heads, :]
    n_near_far = tail // d_far + 1

    for i in range(n_new):
        q = qkv_ref[0, i, 0:n_heads, :] * (HEAD_DIM ** -0.5)
        lists = []
        for d in DILATIONS[:-1]:
            rows = _rows(tail + i - d * SPAN, SPAN + 1, d)
            lists.append((kn_scr[rows], vn_scr[rows]))
        rows = _rows(tail + i - d_far * (n_near_far - 1), n_near_far, d_far)
        lists.append((kn_scr[rows], vn_scr[rows]))
        lists.append((kf_ref[0, :, i], vf_ref[0, :, i]))
        scores = [jnp.sum(k * q[None], axis=-1, keepdims=True) for k, _ in lists]
        m = functools.reduce(jnp.maximum, [jnp.max(s, axis=0) for s in scores])
        den = jnp.zeros((n_heads, 1), F32)
        num = jnp.zeros((n_heads, HEAD_DIM), F32)
        for s, (_, v) in zip(scores, lists):
            p = jnp.exp(s - m[None])
            den = den + jnp.sum(p, axis=0)
            num = num + jnp.sum(p * v, axis=0)
        o_ref[0, i] = num / den


def _attn_sample(qkv, cache_k, cache_v):
    b, n_new, _, _ = qkv.shape
    buf, n_heads = cache_k.shape[1], cache_k.shape[2]
    d_mid, d_far = DILATIONS[1], DILATIONS[2]
    tail = d_mid * SPAN
    assert buf == d_far * SPAN and n_new <= d_mid and tail % d_far == 0
    n_far = (buf - tail) // d_far
    far_shape = (b, buf // d_far, d_far, n_heads, HEAD_DIM)
    near = pl.BlockSpec((1, tail, n_heads, HEAD_DIM), lambda i: (i, buf // tail - 1, 0, 0))
    far = pl.BlockSpec((1, n_far, n_new, n_heads, HEAD_DIM), lambda i: (i, 0, 0, 0, 0))
    new = pl.BlockSpec((1, n_new, 3 * n_heads, HEAD_DIM), lambda i: (i, 0, 0, 0))
    ctx = pltpu.VMEM((tail + 8, n_heads, HEAD_DIM), F32)
    return pl.pallas_call(
        functools.partial(_attn_sample_kernel, n_heads=n_heads),
        grid=(b,),
        in_specs=[new, near, near, far, far],
        out_specs=pl.BlockSpec((1, n_new, n_heads, HEAD_DIM), lambda i: (i, 0, 0, 0)),
        out_shape=jax.ShapeDtypeStruct((b, n_new, n_heads, HEAD_DIM), F32),
        scratch_shapes=[ctx, ctx],
        compiler_params=_params("parallel"),
        name="attn_sample",
    )(qkv, cache_k, cache_v, cache_k.reshape(far_shape), cache_v.reshape(far_shape))


def _ln_swish(y, g, b):
    mu = jnp.mean(y, axis=-1, keepdims=True)
    yc = y - mu
    var = jnp.mean(yc * yc, axis=-1, keepdims=True)
    z = yc * lax.rsqrt(var + LN_EPS) * g + b
    return z * jax.nn.sigmoid(z)


def _conv_prompt_kernel(up_ref, u_ref, w_ref, b_ref, lg_ref, lb_ref, o_ref, ctx_scr, y_scr):
    tm, c = u_ref.shape
    n_slabs = c // LANES
    slab = lambda cs: slice(cs * LANES, (cs + 1) * LANES)

    @pl.when(pl.program_id(0) > 0)
    def _():
        for cs in range(n_slabs):
            ctx_scr[cs, 0:CONV_HALO, :] = up_ref[:, slab(cs)]

    @pl.when(pl.program_id(0) == 0)
    def _():
        for cs in range(n_slabs):
            ctx_scr[cs, 0:CONV_HALO, :] = jnp.zeros((CONV_HALO, LANES), F32)

    for cs in range(n_slabs):
        ctx_scr[cs, CONV_HALO:CONV_HALO + tm, :] = u_ref[:, slab(cs)]

    off = CONV_HALO - (CONV_K - 1)
    rc = 16

    for cs in range(n_slabs):
        taps = [jnp.broadcast_to(w_ref[j:j + 1, slab(cs)], (rc, LANES)) for j in range(CONV_K)]
        bias = jnp.broadcast_to(b_ref[:, slab(cs)], (rc, LANES))

        def chunk(i, carry, cs=cs, taps=taps, bias=bias):
            r0 = i * (2 * rc)
            for phase in range(2):
                acc = bias
                for j in range(CONV_K):
                    acc = acc + ctx_scr[cs, pl.ds(r0 + phase + off + j, rc, stride=2), :] * taps[j]
                y_scr[cs, pl.ds(r0 + phase, rc, stride=2), :] = acc
            return carry

        lax.fori_loop(0, tm // (2 * rc), chunk, 0)

    def ln_chunk(i, carry):
        rows = pl.ds(pl.multiple_of(i * rc, rc), rc)
        ys = [y_scr[cs, rows, :] for cs in range(n_slabs)]
        mu = jnp.sum(functools.reduce(jnp.add, ys), axis=-1, keepdims=True) * (1.0 / c)
        ycs = [y - mu for y in ys]
        var = jnp.sum(functools.reduce(jnp.add, [yc * yc for yc in ycs]), axis=-1, keepdims=True) * (1.0 / c)
        inv = lax.rsqrt(var + LN_EPS)
        for cs in range(n_slabs):
            z = ycs[cs] * inv * lg_ref[:, slab(cs)] + lb_ref[:, slab(cs)]
            o_ref[rows, slab(cs)] = (z * jax.nn.sigmoid(z)).astype(o_ref.dtype)
        return carry

    lax.fori_loop(0, tm // rc, ln_chunk, 0)


def _conv_prompt(u, conv_w, conv_b, ln_g, ln_b):
    s, c = u.shape
    tm = _tile(s, 256, CONV_HALO)
    vec = pl.BlockSpec((1, c), lambda i: (0, 0))
    return pl.pallas_call(
        _conv_prompt_kernel,
        grid=(s // tm,),
        in_specs=[pl.BlockSpec((CONV_HALO, c), lambda i: (jnp.maximum(i * (tm // CONV_HALO) - 1, 0), 0)),
                  pl.BlockSpec((tm, c), lambda i: (i, 0)),
                  pl.BlockSpec((CONV_K, c), lambda i: (0, 0)), vec, vec, vec],
        out_specs=pl.BlockSpec((tm, c), lambda i: (i, 0)),
        out_shape=jax.ShapeDtypeStruct((s, c), BF16),
        scratch_shapes=[pltpu.VMEM((c // LANES, CONV_HALO + tm, LANES), F32),
                        pltpu.VMEM((c // LANES, tm, LANES), F32)],
        compiler_params=_params("parallel"),
        name="conv_prompt",
    )(u, u, conv_w, conv_b, ln_g, ln_b)


def _conv_sample_kernel(st_ref, u_ref, w_ref, b_ref, lg_ref, lb_ref, o_ref, ns_ref, ctx_scr):
    n_state, n_new = st_ref.shape[1], u_ref.shape[1]
    ctx_scr[0:n_state, :] = st_ref[0]
    ctx_scr[n_state:n_state + n_new, :] = u_ref[0]
    acc = jnp.zeros(u_ref.shape[1:], F32)
    for j in range(CONV_K):
        acc = acc + ctx_scr[j:j + n_new, :] * w_ref[j:j + 1, :]
    o_ref[0] = _ln_swish(acc + b_ref[...], lg_ref[...], lb_ref[...])
    ns_ref[0] = ctx_scr[n_new:n_new + n_state, :]


def _conv_sample(state, u, conv_w, conv_b, ln_g, ln_b):
    b, n_state, c = state.shape
    n_new = u.shape[1]
    assert n_state == CONV_K - 1
    vec = pl.BlockSpec((1, c), lambda i: (0, 0))
    return pl.pallas_call(
        _conv_sample_kernel,
        grid=(b,),
        in_specs=[pl.BlockSpec((1, n_state, c), lambda i: (i, 0, 0)),
                  pl.BlockSpec((1, n_new, c), lambda i: (i, 0, 0)),
                  pl.BlockSpec((CONV_K, c), lambda i: (0, 0)), vec, vec, vec],
        out_specs=[pl.BlockSpec((1, n_new, c), lambda i: (i, 0, 0)),
                   pl.BlockSpec((1, n_state, c), lambda i: (i, 0, 0))],
        out_shape=[jax.ShapeDtypeStruct((b, n_new, c), F32),
                   jax.ShapeDtypeStruct((b, n_state, c), F32)],
        scratch_shapes=[pltpu.VMEM((n_state + n_new, c), F32)],
        compiler_params=_params("parallel"),
        name="conv_sample",
    )(state, u, conv_w, conv_b, ln_g, ln_b)


def _outproj_kernel(x_ref, a_ref, c_ref, wa_ref, wc_ref, o_ref):
    acc = _dot(a_ref[...].astype(BF16), wa_ref[...]) + _dot(c_ref[...].astype(BF16), wc_ref[...])
    o_ref[...] = x_ref[...] + acc


def _outproj(x, attn, conv, w_out):
    m, d = x.shape
    a, c = attn.shape[1], conv.shape[1]
    assert a == c
    tm = _tile(m, 512)
    tn = _tile(d, 1024, LANES)
    return pl.pallas_call(
        _outproj_kernel,
        grid=(d // tn, m // tm),
        in_specs=[pl.BlockSpec((tm, tn), lambda n, i: (i, n)),
                  pl.BlockSpec((tm, a), lambda n, i: (i, 0)),
                  pl.BlockSpec((tm, c), lambda n, i: (i, 0)),
                  pl.BlockSpec((a, tn), lambda n, i: (0, n)),
                  pl.BlockSpec((c, tn), lambda n, i: (1, n))],
        out_specs=pl.BlockSpec((tm, tn), lambda n, i: (i, n)),
        out_shape=jax.ShapeDtypeStruct((m, d), F32),
        compiler_params=_params("parallel", "parallel"),
        name="out_proj",
    )(x, attn, conv, w_out, w_out)


def _ffn_kernel(x_ref, g2_ref, wg_ref, wu_ref, wd_ref, gf_ref, y_ref, h_ref):
    f = pl.program_id(1)

    @pl.when(f == 0)
    def _():
        x = x_ref[...]
        h_ref[...] = _rms_to_bf16(x, g2_ref[...])
        y_ref[...] = x

    h = h_ref[...]
    gate = _dot(h, wg_ref[...])
    act = (gate * jax.nn.sigmoid(gate) * _dot(h, wu_ref[...])).astype(BF16)
    y_ref[...] += _dot(act, wd_ref[...])

    @pl.when(f == pl.num_programs(1) - 1)
    def _():
        y = y_ref[...]
        ms = jnp.mean(y * y, axis=-1, keepdims=True)
        y_ref[...] = y * lax.rsqrt(ms + RMS_EPS) * gf_ref[...]


def _ffn(x, g2, w_gate, w_up, w_down, gf):
    m, d = x.shape
    dff = w_gate.shape[1]
    tm = _tile(m, 512)
    tf = _tile(dff, 256, LANES)
    vec = pl.BlockSpec((1, d), lambda i, f: (0, 0))
    return pl.pallas_call(
        _ffn_kernel,
        grid=(m // tm, dff // tf),
        in_specs=[pl.BlockSpec((tm, d), lambda i, f: (i, 0), pipeline_mode=pl.Buffered(1)), vec,
                  pl.BlockSpec((d, tf), lambda i, f: (0, f)),
                  pl.BlockSpec((d, tf), lambda i, f: (0, f)),
                  pl.BlockSpec((tf, d), lambda i, f: (f, 0)), vec],
        out_specs=pl.BlockSpec((tm, d), lambda i, f: (i, 0)),
        out_shape=jax.ShapeDtypeStruct((m, d), F32),
        scratch_shapes=[pltpu.VMEM((tm, d), BF16)],
        compiler_params=_params("parallel", "arbitrary"),
        name="ffn",
    )(x, g2, w_gate, w_up, w_down, gf)


def _cache_roll_kernel(ck_ref, cv_ref, new_ref, nk_ref, nv_ref, sem, *, n_heads):
    n_new = new_ref.shape[1]
    old = nk_ref.shape[1] - n_new
    drop = ck_ref.shape[1] - old
    copies = []
    for i, (src, dst) in enumerate(((ck_ref, nk_ref), (cv_ref, nv_ref))):
        copies.append(pltpu.make_async_copy(src.at[:, pl.ds(drop, old)], dst.at[:, pl.ds(0, old)],
                                            sem.at[2 * i]))
        copies.append(pltpu.make_async_copy(new_ref.at[:, :, pl.ds((i + 1) * n_heads, n_heads)],
                                            dst.at[:, pl.ds(old, n_new)], sem.at[2 * i + 1]))
    for cp in copies:
        cp.start()
    for cp in copies:
        cp.wait()


def _cache_roll(cache_k, cache_v, qkv_new):
    b, buf, n_heads, _ = cache_k.shape
    n_new = qkv_new.shape[1]
    keep = min(MAX_WINDOW, buf + n_new)
    assert n_new < keep
    any_spec = pl.BlockSpec(memory_space=pl.ANY)
    shape = jax.ShapeDtypeStruct((b, keep, n_heads, HEAD_DIM), F32)
    return pl.pallas_call(
        functools.partial(_cache_roll_kernel, n_heads=n_heads),
        in_specs=[any_spec, any_spec, any_spec],
        out_specs=[any_spec, any_spec],
        out_shape=[shape, shape],
        scratch_shapes=[pltpu.SemaphoreType.DMA((4,))],
        name="cache_roll",
    )(cache_k, cache_v, qkv_new)


def kernel(x_prompt, x_sample, cache_k, cache_v, state_conv, norm1_g, w_in, conv_w, conv_b, conv_ln_g,
           conv_ln_b, w_out, norm2_g, w_gate, w_up, w_down, final_g):
    depth = w_in.shape[0]
    bp, s, d = x_prompt.shape
    bs, t_new, _ = x_sample.shape
    assert depth == 1 and bp == 1, "one layer, one prompt sequence"
    attn_w = d // 2
    conv_c = d - attn_w
    n_heads = attn_w // HEAD_DIM
    buf = cache_k.shape[2]

    row = lambda v: v.reshape(1, -1)
    l = 0
    w_in_b, w_out_b = w_in[l].astype(BF16), w_out[l].astype(BF16)
    w_gate_b, w_up_b, w_down_b = w_gate[l].astype(BF16), w_up[l].astype(BF16), w_down[l].astype(BF16)
    g1, g2, gf = row(norm1_g[l]), row(norm2_g[l]), row(final_g)
    cb, lg, lb = row(conv_b[l]), row(conv_ln_g[l]), row(conv_ln_b[l])

    xp = x_prompt.reshape(s, d)
    qkv_p, u_p = _project_in(xp, g1, w_in_b, jnp.arange(s, dtype=jnp.int32), attn_w, conv_c)
    attn_p = _attn_prompt(qkv_p, n_heads)
    conv_p = _conv_prompt(u_p, conv_w[l], cb, lg, lb)
    x1_p = _outproj(xp, attn_p, conv_p, w_out_b)
    y_p = _ffn(x1_p, g2, w_gate_b, w_up_b, w_down_b, gf)
    keep_p = min(MAX_WINDOW, s)
    nk_p = qkv_p[s - keep_p:, attn_w:2 * attn_w].reshape(1, 1, keep_p, n_heads, HEAD_DIM)
    nv_p = qkv_p[s - keep_p:, 2 * attn_w:].reshape(1, 1, keep_p, n_heads, HEAD_DIM)
    n_state = CONV_K - 1
    assert s >= n_state
    nc_p = u_p[s - n_state:].reshape(1, 1, n_state, conv_c)

    ms = bs * t_new
    xs = x_sample.reshape(ms, d)
    pos_s = PAST_LEN + jnp.tile(jnp.arange(t_new, dtype=jnp.int32), bs)
    qkv_s, u_s = _project_in(xs, g1, w_in_b, pos_s, attn_w, conv_c)
    qkv_s = qkv_s.reshape(bs, t_new, 3 * n_heads, HEAD_DIM)
    attn_s = _attn_sample(qkv_s, cache_k[l], cache_v[l])
    conv_s, nc_s = _conv_sample(state_conv[l], u_s.reshape(bs, t_new, conv_c), conv_w[l], cb, lg, lb)
    x1_s = _outproj(xs, attn_s.reshape(ms, attn_w), conv_s.reshape(ms, conv_c), w_out_b)
    y_s = _ffn(x1_s, g2, w_gate_b, w_up_b, w_down_b, gf)
    nk_s, nv_s = _cache_roll(cache_k[l], cache_v[l], qkv_s)

    return (y_p.reshape(1, s, d), y_s.reshape(bs, t_new, d), nk_p, nv_p, nc_p,
            nk_s[None], nv_s[None], nc_s.reshape(1, bs, n_state, conv_c))
```

```python
import functools
import math

import jax
import jax.numpy as jnp
from jax import lax
from jax.experimental import pallas as pl
from jax.experimental.pallas import tpu as pltpu

F32 = jnp.float32
BF16 = jnp.bfloat16

HEAD_DIM = 128
ROPE_DIM = HEAD_DIM // 4
ROPE_THETA = 500000.0
CONV_K = 31
RMS_EPS = 1e-6
LN_EPS = 1e-5
PAST_LEN = 8192
DILATIONS = (1, 4, 16)
SPAN = 128
MAX_WINDOW = 2048
ATTN_TILE = MAX_WINDOW
ATTN_UNROLL = 8
CONV_HALO = 32
FFN_DOWN_CHUNK = 1024
NORM_ROWS = 64
CACHE_ROLL_ROWS = 128

LANES = 128
VMEM_LIMIT_BYTES = 56 * 1024 * 1024


def _params(*sem):
    return pltpu.CompilerParams(dimension_semantics=sem, vmem_limit_bytes=VMEM_LIMIT_BYTES)


def _tile(n, pref, mult=8):
    if n <= pref:
        return n
    for t in range(pref - pref % mult, 0, -mult):
        if n % t == 0:
            return t
    raise ValueError(f"no tile for {n}")


def _rms_to_bf16(x, g):
    ms = jnp.mean(x * x, axis=-1, keepdims=True)
    return (x * lax.rsqrt(ms + RMS_EPS) * g).astype(BF16)


def _dot(a, b):
    return jnp.dot(a, b, preferred_element_type=F32)


def _dot_nt(a, b):
    return lax.dot_general(a, b, (((1,), (1,)), ((), ())), preferred_element_type=F32)


def _qkv_kernel(x_ref, g_ref, w_ref, c_ref, s1_ref, s2_ref, o_ref, h_ref, *, n_rope_tiles):
    n = pl.program_id(1)

    @pl.when(n == 0)
    def _():
        h_ref[...] = _rms_to_bf16(x_ref[...], g_ref[...])

    z = _dot(h_ref[...], w_ref[...])

    @pl.when(n < n_rope_tiles)
    def _():
        c, s1, s2 = c_ref[...], s1_ref[...], s2_ref[...]
        for j in range(z.shape[1] // HEAD_DIM):
            zj = z[:, j * HEAD_DIM:(j + 1) * HEAD_DIM]
            lo = pltpu.roll(zj, ROPE_DIM // 2, 1)
            hi = pltpu.roll(zj, HEAD_DIM - ROPE_DIM // 2, 1)
            o_ref[:, j * HEAD_DIM:(j + 1) * HEAD_DIM] = zj * c + lo * s1 + hi * s2

    @pl.when(n >= n_rope_tiles)
    def _():
        o_ref[...] = z


def _glu_kernel(x_ref, g_ref, wa_ref, wg_ref, u_ref, h_ref):
    @pl.when(pl.program_id(1) == 0)
    def _():
        h_ref[...] = _rms_to_bf16(x_ref[...], g_ref[...])

    h = h_ref[...]
    u_ref[...] = _dot(h, wa_ref[...]) * jax.nn.sigmoid(_dot(h, wg_ref[...]))


def _rope_tables(pos):
    half = ROPE_DIM // 2
    inv = ROPE_THETA ** (-jnp.arange(half, dtype=F32) / half)
    ang = pos.astype(F32)[:, None] * inv[None, :]
    cos, sin = jnp.cos(ang), jnp.sin(ang)
    m = pos.shape[0]
    one = jnp.ones((m, HEAD_DIM - ROPE_DIM), F32)
    zero = jnp.zeros((m, HEAD_DIM - ROPE_DIM), F32)
    zh = jnp.zeros((m, half), F32)
    c = jnp.concatenate([cos, cos, one], axis=1)
    s1 = jnp.concatenate([zh, sin, zero], axis=1)
    s2 = jnp.concatenate([-sin, zh, zero], axis=1)
    return c, s1, s2


def _project_in(x, g1, w_in, pos, attn_w, conv_c):
    m, d = x.shape
    tm = _tile(m, 512)
    tn = _tile(attn_w, 512, LANES)
    c, s1, s2 = _rope_tables(pos)
    tab = pl.BlockSpec((tm, HEAD_DIM), lambda i, n: (i, 0))
    qkv = pl.pallas_call(
        functools.partial(_qkv_kernel, n_rope_tiles=2 * attn_w // tn),
        grid=(m // tm, 3 * attn_w // tn),
        in_specs=[pl.BlockSpec((tm, d), lambda i, n: (i, 0)),
                  pl.BlockSpec((1, d), lambda i, n: (0, 0)),
                  pl.BlockSpec((d, tn), lambda i, n: (0, n)),
                  tab, tab, tab],
        out_specs=pl.BlockSpec((tm, tn), lambda i, n: (i, n)),
        out_shape=jax.ShapeDtypeStruct((m, 3 * attn_w), F32),
        scratch_shapes=[pltpu.VMEM((tm, d), BF16)],
        compiler_params=_params("parallel", "arbitrary"),
        name="qkv_proj",
    )(x, g1, w_in, c, s1, s2)

    tc = _tile(conv_c, 512, LANES)
    a0 = 3 * attn_w // tc
    g0 = (3 * attn_w + conv_c) // tc
    u = pl.pallas_call(
        _glu_kernel,
        grid=(m // tm, conv_c // tc),
        in_specs=[pl.BlockSpec((tm, d), lambda i, n: (i, 0)),
                  pl.BlockSpec((1, d), lambda i, n: (0, 0)),
                  pl.BlockSpec((d, tc), lambda i, n: (0, a0 + n)),
                  pl.BlockSpec((d, tc), lambda i, n: (0, g0 + n))],
        out_specs=pl.BlockSpec((tm, tc), lambda i, n: (i, n)),
        out_shape=jax.ShapeDtypeStruct((m, conv_c), F32),
        scratch_shapes=[pltpu.VMEM((tm, d), BF16)],
        compiler_params=_params("parallel", "arbitrary"),
        name="glu_proj",
    )(x, g1, w_in, w_in)
    return qkv, u


def _rows(start, size, stride):
    return pl.ds(start, size) if stride == 1 else pl.ds(start, size, stride=stride)


def _attn_prompt_kernel(q_ref, kp_ref, kc_ref, vp_ref, vc_ref, o_ref, num_scr, m_scr, l_scr):
    tile = pl.program_id(1)
    t = q_ref.shape[0]
    scale = HEAD_DIM ** -0.5
    qi = lax.broadcasted_iota(jnp.int32, (SPAN, 2 * SPAN), 0)
    kj = lax.broadcasted_iota(jnp.int32, (SPAN, 2 * SPAN), 1)
    band = (kj >= qi) & (kj <= qi + SPAN)
    first_key = jnp.where(tile > 0, 0, SPAN)
    band_first = band & (kj >= first_key)

    def attend(gi, d, q_start, q, k, v, mask):
        s = _dot_nt(q.astype(BF16), k.astype(BF16)) * scale
        s = jnp.where(mask, s, -jnp.inf)
        m = jnp.max(s, axis=-1, keepdims=True)
        p = jnp.exp(s - m)
        l = jnp.sum(p, axis=-1, keepdims=True)
        num = _dot(p.astype(BF16), v.astype(BF16))
        rows = _rows(q_start, SPAN, d)
        num_scr[gi, rows, :] = num
        m_scr[gi, rows, :] = jnp.broadcast_to(m, (SPAN, HEAD_DIM))
        l_scr[gi, rows, :] = jnp.broadcast_to(l, (SPAN, HEAD_DIM))

    def first_block(gi, d, r):
        q = q_ref[_rows(r, SPAN, d), :]
        k = jnp.concatenate([kp_ref[_rows(t - d * SPAN + r, SPAN, d), :],
                             kc_ref[_rows(r, SPAN, d), :]], axis=0)
        v = jnp.concatenate([vp_ref[_rows(t - d * SPAN + r, SPAN, d), :],
                             vc_ref[_rows(r, SPAN, d), :]], axis=0)
        attend(gi, d, r, q, k, v, band_first)

    def later_block(gi, d, r, b):
        q_start = r + d * SPAN * b
        if d == 1:
            q_start = pl.multiple_of(q_start, SPAN)
        k_rows = _rows(q_start - d * SPAN, 2 * SPAN, d)
        attend(gi, d, q_start, q_ref[_rows(q_start, SPAN, d), :], kc_ref[k_rows, :], vc_ref[k_rows, :], band)

    u = ATTN_UNROLL
    for gi, d in enumerate(DILATIONS):
        n_blocks = t // (d * SPAN)
        if n_blocks >= u:
            assert n_blocks % u == 0

            def class_body(r, carry, gi=gi, d=d, n_blocks=n_blocks):
                first_block(gi, d, r)
                for b in range(1, u):
                    later_block(gi, d, r, b)

                def rest(i, c):
                    for j in range(u):
                        later_block(gi, d, r, i * u + j)
                    return c

                if n_blocks > u:
                    lax.fori_loop(1, n_blocks // u, rest, 0)
                return carry

            if d == 1:
                class_body(0, 0)
            else:
                lax.fori_loop(0, d, class_body, 0)
        else:
            per_iter = u // n_blocks
            assert u % n_blocks == 0 and d % per_iter == 0

            def class_group(i, carry, gi=gi, d=d, n_blocks=n_blocks, per_iter=per_iter):
                for j in range(per_iter):
                    r = i * per_iter + j
                    first_block(gi, d, r)
                    for b in range(1, n_blocks):
                        later_block(gi, d, r, b)
                return carry

            lax.fori_loop(0, d // per_iter, class_group, 0)

    def combine(c, carry):
        rows = pl.ds(pl.multiple_of(c * SPAN, SPAN), SPAN)
        ms = [m_scr[g, rows, :] for g in range(len(DILATIONS))]
        m_all = functools.reduce(jnp.maximum, ms)
        num = jnp.zeros((SPAN, HEAD_DIM), F32)
        den = jnp.zeros((SPAN, HEAD_DIM), F32)
        for g in range(len(DILATIONS)):
            sc = jnp.exp(ms[g] - m_all)
            num = num + num_scr[g, rows, :] * sc
            den = den + l_scr[g, rows, :] * sc
        o_ref[rows, :] = (num / den).astype(o_ref.dtype)
        return carry

    lax.fori_loop(0, t // SPAN, combine, 0)


def _attn_prompt(qkv, n_heads):
    s = qkv.shape[0]
    t = ATTN_TILE
    assert s % t == 0, "prompt length must be a multiple of the dilation tile"
    cur = lambda off: pl.BlockSpec((t, HEAD_DIM), lambda h, i: (i, off + h))
    prev = lambda off: pl.BlockSpec((t, HEAD_DIM), lambda h, i: (jnp.maximum(i - 1, 0), off + h))
    return pl.pallas_call(
        _attn_prompt_kernel,
        grid=(n_heads, s // t),
        in_specs=[cur(0), prev(n_heads), cur(n_heads), prev(2 * n_heads), cur(2 * n_heads)],
        out_specs=pl.BlockSpec((t, HEAD_DIM), lambda h, i: (i, h)),
        out_shape=jax.ShapeDtypeStruct((s, n_heads * HEAD_DIM), BF16),
        scratch_shapes=[pltpu.VMEM((len(DILATIONS), t, HEAD_DIM), F32)] * 3,
        compiler_params=_params("parallel", "arbitrary"),
        name="attn_prompt",
    )(qkv, qkv, qkv, qkv, qkv)


def _attn_sample_kernel(qkv_ref, kt_ref, vt_ref, kf_ref, vf_ref, o_ref, kn_scr, vn_scr, *, n_heads):
    n_new = qkv_ref.shape[1]
    tail = kt_ref.shape[1]
    d_far = DILATIONS[-1]
    kn_scr[0:tail] = kt_ref[0]
    vn_scr[0:tail] = vt_ref[0]
    kn_scr[tail:tail + n_new] = qkv_ref[0, :, n_heads:2 * n_heads, :]
    vn_scr[tail:tail + n_new] = qkv_ref[0, :, 2 * n_heads:3 * n_heads, :]
    n_near_far = tail // d_far + 1

    for i in range(n_new):
        q = qkv_ref[0, i, 0:n_heads, :] * (HEAD_DIM ** -0.5)
        lists = []
        for d in DILATIONS[:-1]:
            rows = _rows(tail + i - d * SPAN, SPAN + 1, d)
            lists.append((kn_scr[rows], vn_scr[rows]))
        rows = _rows(tail + i - d_far * (n_near_far - 1), n_near_far, d_far)
        lists.append((kn_scr[rows], vn_scr[rows]))
        lists.append((kf_ref[0, :, i], vf_ref[0, :, i]))
        scores = [jnp.sum(k * q[None], axis=-1, keepdims=True) for k, _ in lists]
        m = functools.reduce(jnp.maximum, [jnp.max(s, axis=0) for s in scores])
        den = jnp.zeros((n_heads, 1), F32)
        num = jnp.zeros((n_heads, HEAD_DIM), F32)
        for s, (_, v) in zip(scores, lists):
            p = jnp.exp(s - m[None])
            den = den + jnp.sum(p, axis=0)
            num = num + jnp.sum(p * v, axis=0)
        o_ref[0, i] = num / den


def _attn_sample(qkv, cache_k, cache_v):
    b, n_new, _, _ = qkv.shape
    buf, n_heads = cache_k.shape[1], cache_k.shape[2]
    d_mid, d_far = DILATIONS[1], DILATIONS[2]
    tail = d_mid * SPAN
    assert buf == d_far * SPAN and n_new <= d_mid and tail % d_far == 0
    n_far = (buf - tail) // d_far
    far_shape = (b, buf // d_far, d_far, n_heads, HEAD_DIM)
    near = pl.BlockSpec((1, tail, n_heads, HEAD_DIM), lambda i: (i, buf // tail - 1, 0, 0))
    far = pl.BlockSpec((1, n_far, n_new, n_heads, HEAD_DIM), lambda i: (i, 0, 0, 0, 0))
    new = pl.BlockSpec((1, n_new, 3 * n_heads, HEAD_DIM), lambda i: (i, 0, 0, 0))
    ctx = pltpu.VMEM((tail + 8, n_heads, HEAD_DIM), F32)
    return pl.pallas_call(
        functools.partial(_attn_sample_kernel, n_heads=n_heads),
        grid=(b,),
        in_specs=[new, near, near, far, far],
        out_specs=pl.BlockSpec((1, n_new, n_heads, HEAD_DIM), lambda i: (i, 0, 0, 0)),
        out_shape=jax.ShapeDtypeStruct((b, n_new, n_heads, HEAD_DIM), F32),
        scratch_shapes=[ctx, ctx],
        compiler_params=_params("parallel"),
        name="attn_sample",
    )(qkv, cache_k, cache_v, cache_k.reshape(far_shape), cache_v.reshape(far_shape))


def _ln_swish(y, g, b):
    mu = jnp.mean(y, axis=-1, keepdims=True)
    yc = y - mu
    var = jnp.mean(yc * yc, axis=-1, keepdims=True)
    z = yc * lax.rsqrt(var + LN_EPS) * g + b
    return z * jax.nn.sigmoid(z)


def _conv_prompt_kernel(up_ref, u_ref, w_ref, b_ref, lg_ref, lb_ref, o_ref, ctx_scr, y_scr):
    tm, c = u_ref.shape
    n_slabs = c // LANES
    slab = lambda cs: slice(cs * LANES, (cs + 1) * LANES)

    @pl.when(pl.program_id(0) > 0)
    def _():
        for cs in range(n_slabs):
            ctx_scr[cs, 0:CONV_HALO, :] = up_ref[:, slab(cs)]

    @pl.when(pl.program_id(0) == 0)
    def _():
        for cs in range(n_slabs):
            ctx_scr[cs, 0:CONV_HALO, :] = jnp.zeros((CONV_HALO, LANES), F32)

    for cs in range(n_slabs):
        ctx_scr[cs, CONV_HALO:CONV_HALO + tm, :] = u_ref[:, slab(cs)]

    off = CONV_HALO - (CONV_K - 1)
    rc = 16

    for cs in range(n_slabs):
        taps = [jnp.broadcast_to(w_ref[j:j + 1, slab(cs)], (rc, LANES)) for j in range(CONV_K)]
        bias = jnp.broadcast_to(b_ref[:, slab(cs)], (rc, LANES))

        def chunk(i, carry, cs=cs, taps=taps, bias=bias):
            r0 = i * (2 * rc)
            for phase in range(2):
                acc = bias
                for j in range(CONV_K):
                    acc = acc + ctx_scr[cs, pl.ds(r0 + phase + off + j, rc, stride=2), :] * taps[j]
                y_scr[cs, pl.ds(r0 + phase, rc, stride=2), :] = acc
            return carry

        lax.fori_loop(0, tm // (2 * rc), chunk, 0)

    def ln_chunk(i, carry):
        rows = pl.ds(pl.multiple_of(i * rc, rc), rc)
        ys = [y_scr[cs, rows, :] for cs in range(n_slabs)]
        mu = jnp.sum(functools.reduce(jnp.add, ys), axis=-1, keepdims=True) * (1.0 / c)
        ycs = [y - mu for y in ys]
        var = jnp.sum(functools.reduce(jnp.add, [yc * yc for yc in ycs]), axis=-1, keepdims=True) * (1.0 / c)
        inv = lax.rsqrt(var + LN_EPS)
        for cs in range(n_slabs):
            z = ycs[cs] * inv * lg_ref[:, slab(cs)] + lb_ref[:, slab(cs)]
            o_ref[rows, slab(cs)] = (z * jax.nn.sigmoid(z)).astype(o_ref.dtype)
        return carry

    lax.fori_loop(0, tm // rc, ln_chunk, 0)


def _conv_prompt(u, conv_w, conv_b, ln_g, ln_b):
    s, c = u.shape
    tm = _tile(s, 256, CONV_HALO)
    vec = pl.BlockSpec((1, c), lambda i: (0, 0))
    return pl.pallas_call(
        _conv_prompt_kernel,
        grid=(s // tm,),
        in_specs=[pl.BlockSpec((CONV_HALO, c), lambda i: (jnp.maximum(i * (tm // CONV_HALO) - 1, 0), 0)),
                  pl.BlockSpec((tm, c), lambda i: (i, 0)),
                  pl.BlockSpec((CONV_K, c), lambda i: (0, 0)), vec, vec, vec],
        out_specs=pl.BlockSpec((tm, c), lambda i: (i, 0)),
        out_shape=jax.ShapeDtypeStruct((s, c), BF16),
        scratch_shapes=[pltpu.VMEM((c // LANES, CONV_HALO + tm, LANES), F32),
                        pltpu.VMEM((c // LANES, tm, LANES), F32)],
        compiler_params=_params("parallel"),
        name="conv_prompt",
    )(u, u, conv_w, conv_b, ln_g, ln_b)


def _conv_sample_kernel(st_ref, u_ref, w_ref, b_ref, lg_ref, lb_ref, o_ref, ns_ref, ctx_scr):
    n_state, n_new = st_ref.shape[1], u_ref.shape[1]
    ctx_scr[0:n_state, :] = st_ref[0]
    ctx_scr[n_state:n_state + n_new, :] = u_ref[0]
    acc = jnp.zeros(u_ref.shape[1:], F32)
    for j in range(CONV_K):
        acc = acc + ctx_scr[j:j + n_new, :] * w_ref[j:j + 1, :]
    o_ref[0] = _ln_swish(acc + b_ref[...], lg_ref[...], lb_ref[...])
    ns_ref[0] = ctx_scr[n_new:n_new + n_state, :]


def _conv_sample(state, u, conv_w, conv_b, ln_g, ln_b):
    b, n_state, c = state.shape
    n_new = u.shape[1]
    assert n_state == CONV_K - 1
    vec = pl.BlockSpec((1, c), lambda i: (0, 0))
    return pl.pallas_call(
        _conv_sample_kernel,
        grid=(b,),
        in_specs=[pl.BlockSpec((1, n_state, c), lambda i: (i, 0, 0)),
                  pl.BlockSpec((1, n_new, c), lambda i: (i, 0, 0)),
                  pl.BlockSpec((CONV_K, c), lambda i: (0, 0)), vec, vec, vec],
        out_specs=[pl.BlockSpec((1, n_new, c), lambda i: (i, 0, 0)),
                   pl.BlockSpec((1, n_state, c), lambda i: (i, 0, 0))],
        out_shape=[jax.ShapeDtypeStruct((b, n_new, c), F32),
                   jax.ShapeDtypeStruct((b, n_state, c), F32)],
        scratch_shapes=[pltpu.VMEM((n_state + n_new, c), F32)],
        compiler_params=_params("parallel"),
        name="conv_sample",
    )(state, u, conv_w, conv_b, ln_g, ln_b)


def _outproj_kernel(x_ref, a_ref, c_ref, wa_ref, wc_ref, o_ref):
    acc = _dot(a_ref[...].astype(BF16), wa_ref[...]) + _dot(c_ref[...].astype(BF16), wc_ref[...])
    o_ref[...] = x_ref[...] + acc


def _outproj(x, attn, conv, w_out):
    m, d = x.shape
    a, c = attn.shape[1], conv.shape[1]
    assert a == c
    tm = _tile(m, 512)
    tn = _tile(d, 1024, LANES)
    return pl.pallas_call(
        _outproj_kernel,
        grid=(d // tn, m // tm),
        in_specs=[pl.BlockSpec((tm, tn), lambda n, i: (i, n)),
                  pl.BlockSpec((tm, a), lambda n, i: (i, 0)),
                  pl.BlockSpec((tm, c), lambda n, i: (i, 0)),
                  pl.BlockSpec((a, tn), lambda n, i: (0, n)),
                  pl.BlockSpec((c, tn), lambda n, i: (1, n))],
        out_specs=pl.BlockSpec((tm, tn), lambda n, i: (i, n)),
        out_shape=jax.ShapeDtypeStruct((m, d), F32),
        compiler_params=_params("parallel", "parallel"),
        name="out_proj",
    )(x, attn, conv, w_out, w_out)


def _ffn_kernel(x_ref, g2_ref, wg_ref, wu_ref, wd_ref, gf_ref, *rest, roll):
    if roll is None:
        y_ref, h_ref = rest
    else:
        ck_ref, cv_ref, new_ref, y_ref, nk_ref, nv_ref, h_ref = rest
    f = pl.program_id(1)
    tm, d = y_ref.shape
    rc = _tile(tm, NORM_ROWS)

    def by_row_chunks(body):
        def step(c, carry):
            body(pl.ds(pl.multiple_of(c * rc, rc), rc))
            return carry
        lax.fori_loop(0, tm // rc, step, 0)

    @pl.when(f == 0)
    def _():
        def body(rows):
            x = x_ref[rows, :]
            h_ref[rows, :] = _rms_to_bf16(x, g2_ref[...])
            y_ref[rows, :] = x
        by_row_chunks(body)

    h = h_ref[...]
    gate = _dot(h, wg_ref[...])
    act = (gate * jax.nn.sigmoid(gate) * _dot(h, wu_ref[...])).astype(BF16)
    dc = _tile(d, FFN_DOWN_CHUNK, LANES)
    for c0 in range(0, d, dc):
        y_ref[:, c0:c0 + dc] += _dot(act, wd_ref[:, c0:c0 + dc])

    @pl.when(f == pl.num_programs(1) - 1)
    def _():
        def body(rows):
            y = y_ref[rows, :]
            ms = jnp.mean(y * y, axis=-1, keepdims=True)
            y_ref[rows, :] = y * lax.rsqrt(ms + RMS_EPS) * gf_ref[...]
        by_row_chunks(body)

    if roll is not None:
        step = pl.program_id(0) * pl.num_programs(1) + f
        _cache_roll_chunk(step, roll, ck_ref, cv_ref, new_ref, nk_ref, nv_ref)


def _cache_roll_chunk(step, roll, ck_ref, cv_ref, new_ref, nk_ref, nv_ref):
    n_chunks, per_seq, n_heads = roll
    n_new = new_ref.shape[1]
    r = nk_ref.shape[1]
    last = step % per_seq == per_seq - 1
    active = step < n_chunks
    for i, (src, dst) in enumerate(((ck_ref, nk_ref), (cv_ref, nv_ref))):
        @pl.when(active & jnp.logical_not(last))
        def _(src=src, dst=dst):
            dst[...] = src[...]

        @pl.when(active & last)
        def _(src=src, dst=dst, i=i):
            dst[0, 0:r - n_new] = src[0, n_new:r]
            dst[0, r - n_new:r] = new_ref[0, :, (i + 1) * n_heads:(i + 2) * n_heads, :]


def _ffn(x, g2, w_gate, w_up, w_down, gf, caches=None):
    m, d = x.shape
    dff = w_gate.shape[1]
    tm = _tile(m, 512)
    tf = _tile(dff, 256, LANES)
    nf = dff // tf
    grid = (m // tm, nf)
    vec = pl.BlockSpec((1, d), lambda i, f: (0, 0))
    in_specs = [pl.BlockSpec((tm, d), lambda i, f: (i, 0), pipeline_mode=pl.Buffered(1)), vec,
                pl.BlockSpec((d, tf), lambda i, f: (0, f)),
                pl.BlockSpec((d, tf), lambda i, f: (0, f)),
                pl.BlockSpec((tf, d), lambda i, f: (f, 0)), vec]
    out_specs = [pl.BlockSpec((tm, d), lambda i, f: (i, 0))]
    out_shape = [jax.ShapeDtypeStruct((m, d), F32)]
    args = [x, g2, w_gate, w_up, w_down, gf]
    roll = None
    if caches is not None:
        cache_k, cache_v, qkv_new = caches
        b, buf, n_heads, _ = cache_k.shape
        n_new = qkv_new.shape[1]
        assert buf == MAX_WINDOW, "the rolled cache keeps exactly the window"
        r = CACHE_ROLL_ROWS
        per_seq = buf // r
        n_chunks = b * per_seq
        assert buf % r == 0 and n_new < r and n_chunks <= grid[0] * grid[1]

        def chunk(i, f):
            c = jnp.minimum(i * nf + f, n_chunks - 1)
            return c // per_seq, c % per_seq

        def src_map(i, f):
            seq, c = chunk(i, f)
            return seq, jnp.minimum(c * r + n_new, buf - r), 0, 0

        src = pl.BlockSpec((pl.Element(1), pl.Element(r), pl.Element(n_heads), pl.Element(HEAD_DIM)), src_map)
        dst = pl.BlockSpec((1, r, n_heads, HEAD_DIM), lambda i, f: (*chunk(i, f), 0, 0))
        new = pl.BlockSpec((1, n_new, 3 * n_heads, HEAD_DIM), lambda i, f: (chunk(i, f)[0], 0, 0, 0))
        in_specs += [src, src, new]
        out_specs += [dst, dst]
        out_shape += [jax.ShapeDtypeStruct((b, buf, n_heads, HEAD_DIM), F32)] * 2
        args += [cache_k, cache_v, qkv_new]
        roll = (n_chunks, per_seq, n_heads)
    out = pl.pallas_call(
        functools.partial(_ffn_kernel, roll=roll),
        grid=grid,
        in_specs=in_specs,
        out_specs=out_specs,
        out_shape=out_shape,
        scratch_shapes=[pltpu.VMEM((tm, d), BF16)],
        compiler_params=_params("arbitrary", "arbitrary"),
        name="ffn",
    )(*args)
    return out[0] if caches is None else out


def kernel(x_prompt, x_sample, cache_k, cache_v, state_conv, norm1_g, w_in, conv_w, conv_b, conv_ln_g,
           conv_ln_b, w_out, norm2_g, w_gate, w_up, w_down, final_g):
    depth = w_in.shape[0]
    bp, s, d = x_prompt.shape
    bs, t_new, _ = x_sample.shape
    assert depth == 1 and bp == 1, "one layer, one prompt sequence"
    attn_w = d // 2
    conv_c = d - attn_w
    n_heads = attn_w // HEAD_DIM
    buf = cache_k.shape[2]

    row = lambda v: v.reshape(1, -1)
    l = 0
    w_in_b, w_out_b = w_in[l].astype(BF16), w_out[l].astype(BF16)
    w_gate_b, w_up_b, w_down_b = w_gate[l].astype(BF16), w_up[l].astype(BF16), w_down[l].astype(BF16)
    g1, g2, gf = row(norm1_g[l]), row(norm2_g[l]), row(final_g)
    cb, lg, lb = row(conv_b[l]), row(conv_ln_g[l]), row(conv_ln_b[l])

    ms = bs * t_new
    xs = x_sample.reshape(ms, d)
    pos_s = PAST_LEN + jnp.tile(jnp.arange(t_new, dtype=jnp.int32), bs)
    qkv_s, u_s = _project_in(xs, g1, w_in_b, pos_s, attn_w, conv_c)
    qkv_s = qkv_s.reshape(bs, t_new, 3 * n_heads, HEAD_DIM)

    xp = x_prompt.reshape(s, d)
    qkv_p, u_p = _project_in(xp, g1, w_in_b, jnp.arange(s, dtype=jnp.int32), attn_w, conv_c)
    attn_p = _attn_prompt(qkv_p, n_heads)
    conv_p = _conv_prompt(u_p, conv_w[l], cb, lg, lb)
    x1_p = _outproj(xp, attn_p, conv_p, w_out_b)
    y_p, nk_s, nv_s = _ffn(x1_p, g2, w_gate_b, w_up_b, w_down_b, gf, caches=(cache_k[l], cache_v[l], qkv_s))
    keep_p = min(MAX_WINDOW, s)
    nk_p = qkv_p[s - keep_p:, attn_w:2 * attn_w].reshape(1, 1, keep_p, n_heads, HEAD_DIM)
    nv_p = qkv_p[s - keep_p:, 2 * attn_w:].reshape(1, 1, keep_p, n_heads, HEAD_DIM)
    n_state = CONV_K - 1
    assert s >= n_state
    nc_p = u_p[s - n_state:].reshape(1, 1, n_state, conv_c)

    attn_s = _attn_sample(qkv_s, cache_k[l], cache_v[l])
    conv_s, nc_s = _conv_sample(state_conv[l], u_s.reshape(bs, t_new, conv_c), conv_w[l], cb, lg, lb)
    x1_s = _outproj(xs, attn_s.reshape(ms, attn_w), conv_s.reshape(ms, conv_c), w_out_b)
    y_s = _ffn(x1_s, g2, w_gate_b, w_up_b, w_down_b, gf)

    return (y_p.reshape(1, s, d), y_s.reshape(bs, t_new, d), nk_p, nv_p, nc_p,
            nk_s[None], nv_s[None], nc_s.reshape(1, bs, n_state, conv_c))
```

```python
import functools
import math

import jax
import jax.numpy as jnp
from jax import lax
from jax.experimental import pallas as pl
from jax.experimental.pallas import tpu as pltpu

F32 = jnp.float32
BF16 = jnp.bfloat16

HEAD_DIM = 128
ROPE_DIM = HEAD_DIM // 4
ROPE_THETA = 500000.0
CONV_K = 31
RMS_EPS = 1e-6
LN_EPS = 1e-5
PAST_LEN = 8192
DILATIONS = (1, 4, 16)
SPAN = 128
MAX_WINDOW = 2048
ATTN_TILE = MAX_WINDOW
ATTN_UNROLL = 8
CONV_HALO = 32
FFN_DOWN_CHUNK = 1024
NORM_ROWS = 64
WEIGHT_CAST_ROWS = 512
CACHE_ROLL_ROWS = 128

LANES = 128
VMEM_LIMIT_BYTES = 56 * 1024 * 1024


def _params(*sem):
    return pltpu.CompilerParams(dimension_semantics=sem, vmem_limit_bytes=VMEM_LIMIT_BYTES)


def _tile(n, pref, mult=8):
    if n <= pref:
        return n
    for t in range(pref - pref % mult, 0, -mult):
        if n % t == 0:
            return t
    raise ValueError(f"no tile for {n}")


def _rms_to_bf16(x, g):
    ms = jnp.mean(x * x, axis=-1, keepdims=True)
    return (x * lax.rsqrt(ms + RMS_EPS) * g).astype(BF16)


def _dot(a, b):
    return jnp.dot(a, b, preferred_element_type=F32)


def _dot_nt(a, b):
    return lax.dot_general(a, b, (((1,), (1,)), ((), ())), preferred_element_type=F32)


def _norm_kernel(x_ref, g_ref, h_ref):
    h_ref[...] = _rms_to_bf16(x_ref[...], g_ref[...])


def _cast_weight(w_ref, w_bf):
    k = w_ref.shape[0]
    rc = _tile(k, WEIGHT_CAST_ROWS)
    for r0 in range(0, k, rc):
        w_bf[r0:r0 + rc, :] = w_ref[r0:r0 + rc, :].astype(BF16)


def _row_parts(tm):
    half = tm // 2
    return ((0, tm),) if half % 16 else ((0, half), (half, tm))


def _qkv_kernel(h_ref, w_ref, tab_ref, o_ref, w_bf):
    @pl.when(pl.program_id(1) == 0)
    def _():
        _cast_weight(w_ref, w_bf)

    for r0, r1 in _row_parts(h_ref.shape[0]):
        z = _dot(h_ref[r0:r1, :], w_bf[...])
        c, s1, s2 = tab_ref[0, 0, r0:r1, :], tab_ref[0, 1, r0:r1, :], tab_ref[0, 2, r0:r1, :]
        for j in range(z.shape[1] // HEAD_DIM):
            zj = z[:, j * HEAD_DIM:(j + 1) * HEAD_DIM]
            lo = pltpu.roll(zj, ROPE_DIM // 2, 1)
            hi = pltpu.roll(zj, HEAD_DIM - ROPE_DIM // 2, 1)
            o_ref[r0:r1, j * HEAD_DIM:(j + 1) * HEAD_DIM] = zj * c + lo * s1 + hi * s2


def _glu_kernel(h_ref, wa_ref, wg_ref, u_ref, wa_bf, wg_bf):
    @pl.when(pl.program_id(1) == 0)
    def _():
        _cast_weight(wa_ref, wa_bf)
        _cast_weight(wg_ref, wg_bf)

    for r0, r1 in _row_parts(h_ref.shape[0]):
        h = h_ref[r0:r1, :]
        u_ref[r0:r1, :] = _dot(h, wa_bf[...]) * jax.nn.sigmoid(_dot(h, wg_bf[...]))


def _rope_tables(pos):
    half = ROPE_DIM // 2
    inv = ROPE_THETA ** (-jnp.arange(half, dtype=F32) / half)
    ang = pos.astype(F32)[:, None] * inv[None, :]
    cos, sin = jnp.cos(ang), jnp.sin(ang)
    m = pos.shape[0]
    one = jnp.ones((m, HEAD_DIM - ROPE_DIM), F32)
    zero = jnp.zeros((m, HEAD_DIM - ROPE_DIM), F32)
    zh = jnp.zeros((m, half), F32)
    c = jnp.concatenate([cos, cos, one], axis=1)
    s1 = jnp.concatenate([zh, sin, zero], axis=1)
    s2 = jnp.concatenate([-sin, zh, zero], axis=1)
    rot = jnp.stack([c, s1, s2])
    ident = jnp.stack([jnp.ones_like(c), jnp.zeros_like(c), jnp.zeros_like(c)])
    return jnp.stack([rot, ident])


def _project_in(x, g1, w_in, pos, attn_w, conv_c):
    m, d = x.shape
    tr = _tile(m, 256)
    h = pl.pallas_call(
        _norm_kernel,
        grid=(m // tr,),
        in_specs=[pl.BlockSpec((tr, d), lambda i: (i, 0)), pl.BlockSpec((1, d), lambda i: (0, 0))],
        out_specs=pl.BlockSpec((tr, d), lambda i: (i, 0)),
        out_shape=jax.ShapeDtypeStruct((m, d), BF16),
        compiler_params=_params("parallel"),
        name="norm1",
    )(x, g1)

    tm = _tile(m, 1024, 16)
    tn = _tile(attn_w, 512, LANES)
    n_rope = 2 * attn_w // tn
    qkv = pl.pallas_call(
        _qkv_kernel,
        grid=(3 * attn_w // tn, m // tm),
        in_specs=[pl.BlockSpec((tm, d), lambda n, i: (i, 0)),
                  pl.BlockSpec((d, tn), lambda n, i: (0, n)),
                  pl.BlockSpec((1, 3, tm, HEAD_DIM), lambda n, i: (jnp.where(n < n_rope, 0, 1), 0, i, 0))],
        out_specs=pl.BlockSpec((tm, tn), lambda n, i: (i, n)),
        out_shape=jax.ShapeDtypeStruct((m, 3 * attn_w), F32),
        scratch_shapes=[pltpu.VMEM((d, tn), BF16)],
        compiler_params=_params("arbitrary", "arbitrary"),
        name="qkv_proj",
    )(h, w_in, _rope_tables(pos))

    tc = _tile(conv_c, 256, LANES)
    a0 = 3 * attn_w // tc
    g0 = (3 * attn_w + conv_c) // tc
    u = pl.pallas_call(
        _glu_kernel,
        grid=(conv_c // tc, m // tm),
        in_specs=[pl.BlockSpec((tm, d), lambda n, i: (i, 0)),
                  pl.BlockSpec((d, tc), lambda n, i: (0, a0 + n)),
                  pl.BlockSpec((d, tc), lambda n, i: (0, g0 + n))],
        out_specs=pl.BlockSpec((tm, tc), lambda n, i: (i, n)),
        out_shape=jax.ShapeDtypeStruct((m, conv_c), F32),
        scratch_shapes=[pltpu.VMEM((d, tc), BF16)] * 2,
        compiler_params=_params("arbitrary", "arbitrary"),
        name="glu_proj",
    )(h, w_in, w_in)
    return qkv, u


def _rows(start, size, stride):
    return pl.ds(start, size) if stride == 1 else pl.ds(start, size, stride=stride)


def _attn_prompt_kernel(q_ref, kp_ref, kc_ref, vp_ref, vc_ref, o_ref, num_scr, m_scr, l_scr):
    tile = pl.program_id(1)
    t = q_ref.shape[0]
    scale = HEAD_DIM ** -0.5
    qi = lax.broadcasted_iota(jnp.int32, (SPAN, 2 * SPAN), 0)
    kj = lax.broadcasted_iota(jnp.int32, (SPAN, 2 * SPAN), 1)
    band = (kj >= qi) & (kj <= qi + SPAN)
    first_key = jnp.where(tile > 0, 0, SPAN)
    band_first = band & (kj >= first_key)

    def attend(gi, d, q_start, q, k, v, mask):
        s = _dot_nt(q.astype(BF16), k.astype(BF16)) * scale
        s = jnp.where(mask, s, -jnp.inf)
        m = jnp.max(s, axis=-1, keepdims=True)
        p = jnp.exp(s - m)
        l = jnp.sum(p, axis=-1, keepdims=True)
        num = _dot(p.astype(BF16), v.astype(BF16))
        rows = _rows(q_start, SPAN, d)
        num_scr[gi, rows, :] = num
        m_scr[gi, rows, :] = jnp.broadcast_to(m, (SPAN, HEAD_DIM))
        l_scr[gi, rows, :] = jnp.broadcast_to(l, (SPAN, HEAD_DIM))

    def first_block(gi, d, r):
        q = q_ref[_rows(r, SPAN, d), :]
        k = jnp.concatenate([kp_ref[_rows(t - d * SPAN + r, SPAN, d), :],
                             kc_ref[_rows(r, SPAN, d), :]], axis=0)
        v = jnp.concatenate([vp_ref[_rows(t - d * SPAN + r, SPAN, d), :],
                             vc_ref[_rows(r, SPAN, d), :]], axis=0)
        attend(gi, d, r, q, k, v, band_first)

    def later_block(gi, d, r, b):
        q_start = r + d * SPAN * b
        if d == 1:
            q_start = pl.multiple_of(q_start, SPAN)
        k_rows = _rows(q_start - d * SPAN, 2 * SPAN, d)
        attend(gi, d, q_start, q_ref[_rows(q_start, SPAN, d), :], kc_ref[k_rows, :], vc_ref[k_rows, :], band)

    u = ATTN_UNROLL
    for gi, d in enumerate(DILATIONS):
        n_blocks = t // (d * SPAN)
        if n_blocks >= u:
            assert n_blocks % u == 0

            def class_body(r, carry, gi=gi, d=d, n_blocks=n_blocks):
                first_block(gi, d, r)
                for b in range(1, u):
                    later_block(gi, d, r, b)

                def rest(i, c):
                    for j in range(u):
                        later_block(gi, d, r, i * u + j)
                    return c

                if n_blocks > u:
                    lax.fori_loop(1, n_blocks // u, rest, 0)
                return carry

            if d == 1:
                class_body(0, 0)
            else:
                lax.fori_loop(0, d, class_body, 0)
        else:
            per_iter = u // n_blocks
            assert u % n_blocks == 0 and d % per_iter == 0

            def class_group(i, carry, gi=gi, d=d, n_blocks=n_blocks, per_iter=per_iter):
                for j in range(per_iter):
                    r = i * per_iter + j
                    first_block(gi, d, r)
                    for b in range(1, n_blocks):
                        later_block(gi, d, r, b)
                return carry

            lax.fori_loop(0, d // per_iter, class_group, 0)

    def combine(c, carry):
        rows = pl.ds(pl.multiple_of(c * SPAN, SPAN), SPAN)
        ms = [m_scr[g, rows, :] for g in range(len(DILATIONS))]
        m_all = functools.reduce(jnp.maximum, ms)
        num = jnp.zeros((SPAN, HEAD_DIM), F32)
        den = jnp.zeros((SPAN, HEAD_DIM), F32)
        for g in range(len(DILATIONS)):
            sc = jnp.exp(ms[g] - m_all)
            num = num + num_scr[g, rows, :] * sc
            den = den + l_scr[g, rows, :] * sc
        o_ref[rows, :] = (num / den).astype(o_ref.dtype)
        return carry

    lax.fori_loop(0, t // SPAN, combine, 0)


def _attn_prompt(qkv, n_heads):
    s = qkv.shape[0]
    t = ATTN_TILE
    assert s % t == 0, "prompt length must be a multiple of the dilation tile"
    cur = lambda off: pl.BlockSpec((t, HEAD_DIM), lambda h, i: (i, off + h))
    prev = lambda off: pl.BlockSpec((t, HEAD_DIM), lambda h, i: (jnp.maximum(i - 1, 0), off + h))
    return pl.pallas_call(
        _attn_prompt_kernel,
        grid=(n_heads, s // t),
        in_specs=[cur(0), prev(n_heads), cur(n_heads), prev(2 * n_heads), cur(2 * n_heads)],
        out_specs=pl.BlockSpec((t, HEAD_DIM), lambda h, i: (i, h)),
        out_shape=jax.ShapeDtypeStruct((s, n_heads * HEAD_DIM), BF16),
        scratch_shapes=[pltpu.VMEM((len(DILATIONS), t, HEAD_DIM), F32)] * 3,
        compiler_params=_params("parallel", "arbitrary"),
        name="attn_prompt",
    )(qkv, qkv, qkv, qkv, qkv)


def _attn_sample_kernel(qkv_ref, kt_ref, vt_ref, kf_ref, vf_ref, o_ref, kn_scr, vn_scr, *, n_heads):
    n_new = qkv_ref.shape[1]
    tail = kt_ref.shape[1]
    d_far = DILATIONS[-1]
    kn_scr[0:tail] = kt_ref[0]
    vn_scr[0:tail] = vt_ref[0]
    kn_scr[tail:tail + n_new] = qkv_ref[0, :, n_heads:2 * n_heads, :]
    vn_scr[tail:tail + n_new] = qkv_ref[0, :, 2 * n_heads:3 * n_heads, :]
    n_near_far = tail // d_far + 1

    for i in range(n_new):
        q = qkv_ref[0, i, 0:n_heads, :] * (HEAD_DIM ** -0.5)
        lists = []
        for d in DILATIONS[:-1]:
            rows = _rows(tail + i - d * SPAN, SPAN + 1, d)
            lists.append((kn_scr[rows], vn_scr[rows]))
        rows = _rows(tail + i - d_far * (n_near_far - 1), n_near_far, d_far)
        lists.append((kn_scr[rows], vn_scr[rows]))
        lists.append((kf_ref[0, :, i], vf_ref[0, :, i]))
        scores = [jnp.sum(k * q[None], axis=-1, keepdims=True) for k, _ in lists]
        m = functools.reduce(jnp.maximum, [jnp.max(s, axis=0) for s in scores])
        den = jnp.zeros((n_heads, 1), F32)
        num = jnp.zeros((n_heads, HEAD_DIM), F32)
        for s, (_, v) in zip(scores, lists):
            p = jnp.exp(s - m[None])
            den = den + jnp.sum(p, axis=0)
            num = num + jnp.sum(p * v, axis=0)
        o_ref[0, i] = num / den


def _attn_sample(qkv, cache_k, cache_v):
    b, n_new, _, _ = qkv.shape
    buf, n_heads = cache_k.shape[1], cache_k.shape[2]
    d_mid, d_far = DILATIONS[1], DILATIONS[2]
    tail = d_mid * SPAN
    assert buf == d_far * SPAN and n_new <= d_mid and tail % d_far == 0
    n_far = (buf - tail) // d_far
    far_shape = (b, buf // d_far, d_far, n_heads, HEAD_DIM)
    near = pl.BlockSpec((1, tail, n_heads, HEAD_DIM), lambda i: (i, buf // tail - 1, 0, 0))
    far = pl.BlockSpec((1, n_far, n_new, n_heads, HEAD_DIM), lambda i: (i, 0, 0, 0, 0))
    new = pl.BlockSpec((1, n_new, 3 * n_heads, HEAD_DIM), lambda i: (i, 0, 0, 0))
    ctx = pltpu.VMEM((tail + 8, n_heads, HEAD_DIM), F32)
    return pl.pallas_call(
        functools.partial(_attn_sample_kernel, n_heads=n_heads),
        grid=(b,),
        in_specs=[new, near, near, far, far],
        out_specs=pl.BlockSpec((1, n_new, n_heads, HEAD_DIM), lambda i: (i, 0, 0, 0)),
        out_shape=jax.ShapeDtypeStruct((b, n_new, n_heads, HEAD_DIM), F32),
        scratch_shapes=[ctx, ctx],
        compiler_params=_params("parallel"),
        name="attn_sample",
    )(qkv, cache_k, cache_v, cache_k.reshape(far_shape), cache_v.reshape(far_shape))


def _ln_swish(y, g, b):
    mu = jnp.mean(y, axis=-1, keepdims=True)
    yc = y - mu
    var = jnp.mean(yc * yc, axis=-1, keepdims=True)
    z = yc * lax.rsqrt(var + LN_EPS) * g + b
    return z * jax.nn.sigmoid(z)


def _conv_prompt_kernel(up_ref, u_ref, w_ref, b_ref, lg_ref, lb_ref, o_ref, ctx_scr, y_scr):
    tm, c = u_ref.shape
    n_slabs = c // LANES
    slab = lambda cs: slice(cs * LANES, (cs + 1) * LANES)

    @pl.when(pl.program_id(0) > 0)
    def _():
        for cs in range(n_slabs):
            ctx_scr[cs, 0:CONV_HALO, :] = up_ref[:, slab(cs)]

    @pl.when(pl.program_id(0) == 0)
    def _():
        for cs in range(n_slabs):
            ctx_scr[cs, 0:CONV_HALO, :] = jnp.zeros((CONV_HALO, LANES), F32)

    for cs in range(n_slabs):
        ctx_scr[cs, CONV_HALO:CONV_HALO + tm, :] = u_ref[:, slab(cs)]

    off = CONV_HALO - (CONV_K - 1)
    rc = 32

    for cs in range(n_slabs):
        taps = [jnp.broadcast_to(w_ref[j:j + 1, slab(cs)], (rc, LANES)) for j in range(CONV_K)]
        bias = jnp.broadcast_to(b_ref[:, slab(cs)], (rc, LANES))

        def chunk(i, carry, cs=cs, taps=taps, bias=bias):
            r0 = i * (2 * rc)
            for phase in range(2):
                acc = bias
                for j in range(CONV_K):
                    acc = acc + ctx_scr[cs, pl.ds(r0 + phase + off + j, rc, stride=2), :] * taps[j]
                y_scr[cs, pl.ds(r0 + phase, rc, stride=2), :] = acc
            return carry

        lax.fori_loop(0, tm // (2 * rc), chunk, 0)

    ln_rows = 16

    def ln_chunk(i, carry):
        rows = pl.ds(pl.multiple_of(i * ln_rows, ln_rows), ln_rows)
        ys = [y_scr[cs, rows, :] for cs in range(n_slabs)]
        mu = jnp.sum(functools.reduce(jnp.add, ys), axis=-1, keepdims=True) * (1.0 / c)
        ycs = [y - mu for y in ys]
        var = jnp.sum(functools.reduce(jnp.add, [yc * yc for yc in ycs]), axis=-1, keepdims=True) * (1.0 / c)
        inv = lax.rsqrt(var + LN_EPS)
        for cs in range(n_slabs):
            z = ycs[cs] * inv * lg_ref[:, slab(cs)] + lb_ref[:, slab(cs)]
            o_ref[rows, slab(cs)] = (z * jax.nn.sigmoid(z)).astype(o_ref.dtype)
        return carry

    lax.fori_loop(0, tm // ln_rows, ln_chunk, 0)


def _conv_prompt(u, conv_w, conv_b, ln_g, ln_b):
    s, c = u.shape
    tm = _tile(s, 256, CONV_HALO)
    vec = pl.BlockSpec((1, c), lambda i: (0, 0))
    return pl.pallas_call(
        _conv_prompt_kernel,
        grid=(s // tm,),
        in_specs=[pl.BlockSpec((CONV_HALO, c), lambda i: (jnp.maximum(i * (tm // CONV_HALO) - 1, 0), 0)),
                  pl.BlockSpec((tm, c), lambda i: (i, 0)),
                  pl.BlockSpec((CONV_K, c), lambda i: (0, 0)), vec, vec, vec],
        out_specs=pl.BlockSpec((tm, c), lambda i: (i, 0)),
        out_shape=jax.ShapeDtypeStruct((s, c), BF16),
        scratch_shapes=[pltpu.VMEM((c // LANES, CONV_HALO + tm, LANES), F32),
                        pltpu.VMEM((c // LANES, tm, LANES), F32)],
        compiler_params=_params("parallel"),
        name="conv_prompt",
    )(u, u, conv_w, conv_b, ln_g, ln_b)


def _conv_sample_kernel(st_ref, u_ref, w_ref, b_ref, lg_ref, lb_ref, o_ref, ns_ref, ctx_scr):
    n_state, n_new = st_ref.shape[1], u_ref.shape[1]
    ctx_scr[0:n_state, :] = st_ref[0]
    ctx_scr[n_state:n_state + n_new, :] = u_ref[0]
    acc = jnp.zeros(u_ref.shape[1:], F32)
    for j in range(CONV_K):
        acc = acc + ctx_scr[j:j + n_new, :] * w_ref[j:j + 1, :]
    o_ref[0] = _ln_swish(acc + b_ref[...], lg_ref[...], lb_ref[...])
    ns_ref[0] = ctx_scr[n_new:n_new + n_state, :]


def _conv_sample(state, u, conv_w, conv_b, ln_g, ln_b):
    b, n_state, c = state.shape
    n_new = u.shape[1]
    assert n_state == CONV_K - 1
    vec = pl.BlockSpec((1, c), lambda i: (0, 0))
    return pl.pallas_call(
        _conv_sample_kernel,
        grid=(b,),
        in_specs=[pl.BlockSpec((1, n_state, c), lambda i: (i, 0, 0)),
                  pl.BlockSpec((1, n_new, c), lambda i: (i, 0, 0)),
                  pl.BlockSpec((CONV_K, c), lambda i: (0, 0)), vec, vec, vec],
        out_specs=[pl.BlockSpec((1, n_new, c), lambda i: (i, 0, 0)),
                   pl.BlockSpec((1, n_state, c), lambda i: (i, 0, 0))],
        out_shape=[jax.ShapeDtypeStruct((b, n_new, c), F32),
                   jax.ShapeDtypeStruct((b, n_state, c), F32)],
        scratch_shapes=[pltpu.VMEM((n_state + n_new, c), F32)],
        compiler_params=_params("parallel"),
        name="conv_sample",
    )(state, u, conv_w, conv_b, ln_g, ln_b)


def _outproj_kernel(x_ref, a_ref, c_ref, wa_ref, wc_ref, o_ref, wa_bf, wc_bf):
    @pl.when(pl.program_id(1) == 0)
    def _():
        _cast_weight(wa_ref, wa_bf)
        _cast_weight(wc_ref, wc_bf)

    for r0, r1 in _row_parts(x_ref.shape[0]):
        acc = _dot(a_ref[r0:r1, :].astype(BF16), wa_bf[...]) + _dot(c_ref[r0:r1, :].astype(BF16), wc_bf[...])
        o_ref[r0:r1, :] = x_ref[r0:r1, :] + acc


def _outproj(x, attn, conv, w_out):
    m, d = x.shape
    a, c = attn.shape[1], conv.shape[1]
    assert a == c
    tm = _tile(m, 1024, 16)
    tn = _tile(d, 512, LANES)
    return pl.pallas_call(
        _outproj_kernel,
        grid=(d // tn, m // tm),
        in_specs=[pl.BlockSpec((tm, tn), lambda n, i: (i, n)),
                  pl.BlockSpec((tm, a), lambda n, i: (i, 0)),
                  pl.BlockSpec((tm, c), lambda n, i: (i, 0)),
                  pl.BlockSpec((a, tn), lambda n, i: (0, n)),
                  pl.BlockSpec((c, tn), lambda n, i: (1, n))],
        out_specs=pl.BlockSpec((tm, tn), lambda n, i: (i, n)),
        out_shape=jax.ShapeDtypeStruct((m, d), F32),
        scratch_shapes=[pltpu.VMEM((a, tn), BF16), pltpu.VMEM((c, tn), BF16)],
        compiler_params=_params("arbitrary", "arbitrary"),
        name="out_proj",
    )(x, attn, conv, w_out, w_out)


def _ffn_kernel(x_ref, g2_ref, wg_ref, wu_ref, wd_ref, gf_ref, *rest, roll):
    if roll is None:
        y_ref, h_ref = rest
    else:
        ck_ref, cv_ref, new_ref, y_ref, nk_ref, nv_ref, h_ref = rest
    f = pl.program_id(1)
    tm, d = y_ref.shape
    rc = _tile(tm, NORM_ROWS)

    def by_row_chunks(body):
        def step(c, carry):
            body(pl.ds(pl.multiple_of(c * rc, rc), rc))
            return carry
        lax.fori_loop(0, tm // rc, step, 0)

    @pl.when(f == 0)
    def _():
        def body(rows):
            x = x_ref[rows, :]
            h_ref[rows, :] = _rms_to_bf16(x, g2_ref[...])
            y_ref[rows, :] = x
        by_row_chunks(body)

    h = h_ref[...]
    gate = _dot(h, wg_ref[...])
    act = (gate * jax.nn.sigmoid(gate) * _dot(h, wu_ref[...])).astype(BF16)
    dc = _tile(d, FFN_DOWN_CHUNK, LANES)
    for c0 in range(0, d, dc):
        y_ref[:, c0:c0 + dc] += _dot(act, wd_ref[:, c0:c0 + dc])

    @pl.when(f == pl.num_programs(1) - 1)
    def _():
        def body(rows):
            y = y_ref[rows, :]
            ms = jnp.mean(y * y, axis=-1, keepdims=True)
            y_ref[rows, :] = y * lax.rsqrt(ms + RMS_EPS) * gf_ref[...]
        by_row_chunks(body)

    if roll is not None:
        step = pl.program_id(0) * pl.num_programs(1) + f
        _cache_roll_chunk(step, roll, ck_ref, cv_ref, new_ref, nk_ref, nv_ref)


def _cache_roll_chunk(step, roll, ck_ref, cv_ref, new_ref, nk_ref, nv_ref):
    n_chunks, per_seq, n_heads = roll
    n_new = new_ref.shape[1]
    r = nk_ref.shape[1]
    last = step % per_seq == per_seq - 1
    active = step < n_chunks
    for i, (src, dst) in enumerate(((ck_ref, nk_ref), (cv_ref, nv_ref))):
        @pl.when(active & jnp.logical_not(last))
        def _(src=src, dst=dst):
            dst[...] = src[...]

        @pl.when(active & last)
        def _(src=src, dst=dst, i=i):
            dst[0, 0:r - n_new] = src[0, n_new:r]
            dst[0, r - n_new:r] = new_ref[0, :, (i + 1) * n_heads:(i + 2) * n_heads, :]


def _ffn(x, g2, w_gate, w_up, w_down, gf, caches=None):
    m, d = x.shape
    dff = w_gate.shape[1]
    tm = _tile(m, 512)
    tf = _tile(dff, 256, LANES)
    nf = dff // tf
    grid = (m // tm, nf)
    vec = pl.BlockSpec((1, d), lambda i, f: (0, 0))
    in_specs = [pl.BlockSpec((tm, d), lambda i, f: (i, 0), pipeline_mode=pl.Buffered(1)), vec,
                pl.BlockSpec((d, tf), lambda i, f: (0, f)),
                pl.BlockSpec((d, tf), lambda i, f: (0, f)),
                pl.BlockSpec((tf, d), lambda i, f: (f, 0)), vec]
    out_specs = [pl.BlockSpec((tm, d), lambda i, f: (i, 0))]
    out_shape = [jax.ShapeDtypeStruct((m, d), F32)]
    args = [x, g2, w_gate, w_up, w_down, gf]
    roll = None
    if caches is not None:
        cache_k, cache_v, qkv_new = caches
        b, buf, n_heads, _ = cache_k.shape
        n_new = qkv_new.shape[1]
        assert buf == MAX_WINDOW, "the rolled cache keeps exactly the window"
        r = CACHE_ROLL_ROWS
        per_seq = buf // r
        n_chunks = b * per_seq
        assert buf % r == 0 and n_new < r and n_chunks <= grid[0] * grid[1]

        def chunk(i, f):
            c = jnp.minimum(i * nf + f, n_chunks - 1)
            return c // per_seq, c % per_seq

        def src_map(i, f):
            seq, c = chunk(i, f)
            return seq, jnp.minimum(c * r + n_new, buf - r), 0, 0

        src = pl.BlockSpec((pl.Element(1), pl.Element(r), pl.Element(n_heads), pl.Element(HEAD_DIM)), src_map)
        dst = pl.BlockSpec((1, r, n_heads, HEAD_DIM), lambda i, f: (*chunk(i, f), 0, 0))
        new = pl.BlockSpec((1, n_new, 3 * n_heads, HEAD_DIM), lambda i, f: (chunk(i, f)[0], 0, 0, 0))
        in_specs += [src, src, new]
        out_specs += [dst, dst]
        out_shape += [jax.ShapeDtypeStruct((b, buf, n_heads, HEAD_DIM), F32)] * 2
        args += [cache_k, cache_v, qkv_new]
        roll = (n_chunks, per_seq, n_heads)
    out = pl.pallas_call(
        functools.partial(_ffn_kernel, roll=roll),
        grid=grid,
        in_specs=in_specs,
        out_specs=out_specs,
        out_shape=out_shape,
        scratch_shapes=[pltpu.VMEM((tm, d), BF16)],
        compiler_params=_params("arbitrary", "arbitrary"),
        name="ffn",
    )(*args)
    return out[0] if caches is None else out


def kernel(x_prompt, x_sample, cache_k, cache_v, state_conv, norm1_g, w_in, conv_w, conv_b, conv_ln_g,
           conv_ln_b, w_out, norm2_g, w_gate, w_up, w_down, final_g):
    depth = w_in.shape[0]
    bp, s, d = x_prompt.shape
    bs, t_new, _ = x_sample.shape
    assert depth == 1 and bp == 1, "one layer, one prompt sequence"
    attn_w = d // 2
    conv_c = d - attn_w
    n_heads = attn_w // HEAD_DIM
    buf = cache_k.shape[2]

    row = lambda v: v.reshape(1, -1)
    l = 0
    w_in_b, w_out_b = w_in[l], w_out[l]
    w_gate_b, w_up_b, w_down_b = w_gate[l].astype(BF16), w_up[l].astype(BF16), w_down[l].astype(BF16)
    g1, g2, gf = row(norm1_g[l]), row(norm2_g[l]), row(final_g)
    cb, lg, lb = row(conv_b[l]), row(conv_ln_g[l]), row(conv_ln_b[l])

    ms = bs * t_new
    xs = x_sample.reshape(ms, d)
    pos_s = PAST_LEN + jnp.tile(jnp.arange(t_new, dtype=jnp.int32), bs)
    qkv_s, u_s = _project_in(xs, g1, w_in_b, pos_s, attn_w, conv_c)
    qkv_s = qkv_s.reshape(bs, t_new, 3 * n_heads, HEAD_DIM)

    xp = x_prompt.reshape(s, d)
    qkv_p, u_p = _project_in(xp, g1, w_in_b, jnp.arange(s, dtype=jnp.int32), attn_w, conv_c)
    attn_p = _attn_prompt(qkv_p, n_heads)
    conv_p = _conv_prompt(u_p, conv_w[l], cb, lg, lb)
    x1_p = _outproj(xp, attn_p, conv_p, w_out_b)
    y_p, nk_s, nv_s = _ffn(x1_p, g2, w_gate_b, w_up_b, w_down_b, gf, caches=(cache_k[l], cache_v[l], qkv_s))
    keep_p = min(MAX_WINDOW, s)
    nk_p = qkv_p[s - keep_p:, attn_w:2 * attn_w].reshape(1, 1, keep_p, n_heads, HEAD_DIM)
    nv_p = qkv_p[s - keep_p:, 2 * attn_w:].reshape(1, 1, keep_p, n_heads, HEAD_DIM)
    n_state = CONV_K - 1
    assert s >= n_state
    nc_p = u_p[s - n_state:].reshape(1, 1, n_state, conv_c)

    attn_s = _attn_sample(qkv_s, cache_k[l], cache_v[l])
    conv_s, nc_s = _conv_sample(state_conv[l], u_s.reshape(bs, t_new, conv_c), conv_w[l], cb, lg, lb)
    x1_s = _outproj(xs, attn_s.reshape(ms, attn_w), conv_s.reshape(ms, conv_c), w_out_b)
    y_s = _ffn(x1_s, g2, w_gate_b, w_up_b, w_down_b, gf)

    return (y_p.reshape(1, s, d), y_s.reshape(bs, t_new, d), nk_p, nv_p, nc_p,
            nk_s[None], nv_s[None], nc_s.reshape(1, bs, n_state, conv_c))
```

```python
import functools
import math

import jax
import jax.numpy as jnp
from jax import lax
from jax.experimental import pallas as pl
from jax.experimental.pallas import tpu as pltpu

F32 = jnp.float32
BF16 = jnp.bfloat16

HEAD_DIM = 128
ROPE_DIM = HEAD_DIM // 4
ROPE_THETA = 500000.0
CONV_K = 31
RMS_EPS = 1e-6
LN_EPS = 1e-5
PAST_LEN = 8192
DILATIONS = (1, 4, 16)
SPAN = 128
MAX_WINDOW = 2048
ATTN_TILE = MAX_WINDOW
ATTN_UNROLL = 8
CONV_HALO = 32
FFN_DOWN_CHUNK = 1024
NORM_ROWS = 64
WEIGHT_CAST_ROWS = 512
CACHE_ROLL_ROWS = 128

LANES = 128
VMEM_LIMIT_BYTES = 56 * 1024 * 1024


def _params(*sem):
    return pltpu.CompilerParams(dimension_semantics=sem, vmem_limit_bytes=VMEM_LIMIT_BYTES)


def _tile(n, pref, mult=8):
    if n <= pref:
        return n
    for t in range(pref - pref % mult, 0, -mult):
        if n % t == 0:
            return t
    raise ValueError(f"no tile for {n}")


def _rms_to_bf16(x, g):
    ms = jnp.mean(x * x, axis=-1, keepdims=True)
    return (x * lax.rsqrt(ms + RMS_EPS) * g).astype(BF16)


def _dot(a, b):
    return jnp.dot(a, b, preferred_element_type=F32)


def _dot_nt(a, b):
    return lax.dot_general(a, b, (((1,), (1,)), ((), ())), preferred_element_type=F32)


def _norm_kernel(x_ref, g_ref, h_ref):
    h_ref[...] = _rms_to_bf16(x_ref[...], g_ref[...])


def _cast_weight(w_ref, w_bf):
    k = w_ref.shape[0]
    rc = _tile(k, WEIGHT_CAST_ROWS)
    for r0 in range(0, k, rc):
        w_bf[r0:r0 + rc, :] = w_ref[r0:r0 + rc, :].astype(BF16)


def _row_parts(tm):
    half = tm // 2
    return ((0, tm),) if half % 16 else ((0, half), (half, tm))


def _qkv_kernel(h_ref, w_ref, tab_ref, o_ref, w_bf):
    @pl.when(pl.program_id(1) == 0)
    def _():
        _cast_weight(w_ref, w_bf)

    for r0, r1 in _row_parts(h_ref.shape[0]):
        z = _dot(h_ref[r0:r1, :], w_bf[...])
        c, s1, s2 = tab_ref[0, 0, r0:r1, :], tab_ref[0, 1, r0:r1, :], tab_ref[0, 2, r0:r1, :]
        for j in range(z.shape[1] // HEAD_DIM):
            zj = z[:, j * HEAD_DIM:(j + 1) * HEAD_DIM]
            lo = pltpu.roll(zj, ROPE_DIM // 2, 1)
            hi = pltpu.roll(zj, HEAD_DIM - ROPE_DIM // 2, 1)
            o_ref[r0:r1, j * HEAD_DIM:(j + 1) * HEAD_DIM] = zj * c + lo * s1 + hi * s2


def _glu_kernel(h_ref, wa_ref, wg_ref, u_ref, wa_bf, wg_bf):
    @pl.when(pl.program_id(1) == 0)
    def _():
        _cast_weight(wa_ref, wa_bf)
        _cast_weight(wg_ref, wg_bf)

    for r0, r1 in _row_parts(h_ref.shape[0]):
        h = h_ref[r0:r1, :]
        u_ref[r0:r1, :] = _dot(h, wa_bf[...]) * jax.nn.sigmoid(_dot(h, wg_bf[...]))


def _rope_tables(pos):
    half = ROPE_DIM // 2
    inv = ROPE_THETA ** (-jnp.arange(half, dtype=F32) / half)
    ang = pos.astype(F32)[:, None] * inv[None, :]
    cos, sin = jnp.cos(ang), jnp.sin(ang)
    m = pos.shape[0]
    one = jnp.ones((m, HEAD_DIM - ROPE_DIM), F32)
    zero = jnp.zeros((m, HEAD_DIM - ROPE_DIM), F32)
    zh = jnp.zeros((m, half), F32)
    c = jnp.concatenate([cos, cos, one], axis=1)
    s1 = jnp.concatenate([zh, sin, zero], axis=1)
    s2 = jnp.concatenate([-sin, zh, zero], axis=1)
    rot = jnp.stack([c, s1, s2])
    ident = jnp.stack([jnp.ones_like(c), jnp.zeros_like(c), jnp.zeros_like(c)])
    return jnp.stack([rot, ident])


def _project_in(x, g1, w_in, pos, attn_w, conv_c):
    m, d = x.shape
    tr = _tile(m, 256)
    h = pl.pallas_call(
        _norm_kernel,
        grid=(m // tr,),
        in_specs=[pl.BlockSpec((tr, d), lambda i: (i, 0)), pl.BlockSpec((1, d), lambda i: (0, 0))],
        out_specs=pl.BlockSpec((tr, d), lambda i: (i, 0)),
        out_shape=jax.ShapeDtypeStruct((m, d), BF16),
        compiler_params=_params("parallel"),
        name="norm1",
    )(x, g1)

    tm = _tile(m, 1024, 16)
    tn = _tile(attn_w, 1024, LANES)
    once = pl.Buffered(1)
    n_rope = 2 * attn_w // tn
    qkv = pl.pallas_call(
        _qkv_kernel,
        grid=(3 * attn_w // tn, m // tm),
        in_specs=[pl.BlockSpec((tm, d), lambda n, i: (i, 0)),
                  pl.BlockSpec((d, tn), lambda n, i: (0, n), pipeline_mode=once),
                  pl.BlockSpec((1, 3, tm, HEAD_DIM), lambda n, i: (jnp.where(n < n_rope, 0, 1), 0, i, 0))],
        out_specs=pl.BlockSpec((tm, tn), lambda n, i: (i, n)),
        out_shape=jax.ShapeDtypeStruct((m, 3 * attn_w), F32),
        scratch_shapes=[pltpu.VMEM((d, tn), BF16)],
        compiler_params=_params("arbitrary", "arbitrary"),
        name="qkv_proj",
    )(h, w_in, _rope_tables(pos))

    tc = _tile(conv_c, 512, LANES)
    a0 = 3 * attn_w // tc
    g0 = (3 * attn_w + conv_c) // tc
    u = pl.pallas_call(
        _glu_kernel,
        grid=(conv_c // tc, m // tm),
        in_specs=[pl.BlockSpec((tm, d), lambda n, i: (i, 0)),
                  pl.BlockSpec((d, tc), lambda n, i: (0, a0 + n), pipeline_mode=once),
                  pl.BlockSpec((d, tc), lambda n, i: (0, g0 + n), pipeline_mode=once)],
        out_specs=pl.BlockSpec((tm, tc), lambda n, i: (i, n)),
        out_shape=jax.ShapeDtypeStruct((m, conv_c), F32),
        scratch_shapes=[pltpu.VMEM((d, tc), BF16)] * 2,
        compiler_params=_params("arbitrary", "arbitrary"),
        name="glu_proj",
    )(h, w_in, w_in)
    return qkv, u


def _rows(start, size, stride):
    return pl.ds(start, size) if stride == 1 else pl.ds(start, size, stride=stride)


def _attn_prompt_kernel(q_ref, kp_ref, kc_ref, vp_ref, vc_ref, o_ref, num_scr, m_scr, l_scr):
    tile = pl.program_id(1)
    t = q_ref.shape[0]
    scale = HEAD_DIM ** -0.5
    qi = lax.broadcasted_iota(jnp.int32, (SPAN, 2 * SPAN), 0)
    kj = lax.broadcasted_iota(jnp.int32, (SPAN, 2 * SPAN), 1)
    band = (kj >= qi) & (kj <= qi + SPAN)
    first_key = jnp.where(tile > 0, 0, SPAN)
    band_first = band & (kj >= first_key)

    def attend(gi, d, q_start, q, k, v, mask):
        s = _dot_nt(q.astype(BF16), k.astype(BF16)) * scale
        s = jnp.where(mask, s, -jnp.inf)
        m = jnp.max(s, axis=-1, keepdims=True)
        p = jnp.exp(s - m)
        l = jnp.sum(p, axis=-1, keepdims=True)
        num = _dot(p.astype(BF16), v.astype(BF16))
        rows = _rows(q_start, SPAN, d)
        num_scr[gi, rows, :] = num
        m_scr[gi, rows, :] = jnp.broadcast_to(m, (SPAN, HEAD_DIM))
        l_scr[gi, rows, :] = jnp.broadcast_to(l, (SPAN, HEAD_DIM))

    def first_block(gi, d, r):
        q = q_ref[_rows(r, SPAN, d), :]
        k = jnp.concatenate([kp_ref[_rows(t - d * SPAN + r, SPAN, d), :],
                             kc_ref[_rows(r, SPAN, d), :]], axis=0)
        v = jnp.concatenate([vp_ref[_rows(t - d * SPAN + r, SPAN, d), :],
                             vc_ref[_rows(r, SPAN, d), :]], axis=0)
        attend(gi, d, r, q, k, v, band_first)

    def later_block(gi, d, r, b):
        q_start = r + d * SPAN * b
        if d == 1:
            q_start = pl.multiple_of(q_start, SPAN)
        k_rows = _rows(q_start - d * SPAN, 2 * SPAN, d)
        attend(gi, d, q_start, q_ref[_rows(q_start, SPAN, d), :], kc_ref[k_rows, :], vc_ref[k_rows, :], band)

    u = ATTN_UNROLL
    for gi, d in enumerate(DILATIONS):
        n_blocks = t // (d * SPAN)
        if n_blocks >= u:
            assert n_blocks % u == 0

            def class_body(r, carry, gi=gi, d=d, n_blocks=n_blocks):
                first_block(gi, d, r)
                for b in range(1, u):
                    later_block(gi, d, r, b)

                def rest(i, c):
                    for j in range(u):
                        later_block(gi, d, r, i * u + j)
                    return c

                if n_blocks > u:
                    lax.fori_loop(1, n_blocks // u, rest, 0)
                return carry

            if d == 1:
                class_body(0, 0)
            else:
                lax.fori_loop(0, d, class_body, 0)
        else:
            per_iter = u // n_blocks
            assert u % n_blocks == 0 and d % per_iter == 0

            def class_group(i, carry, gi=gi, d=d, n_blocks=n_blocks, per_iter=per_iter):
                for j in range(per_iter):
                    r = i * per_iter + j
                    first_block(gi, d, r)
                    for b in range(1, n_blocks):
                        later_block(gi, d, r, b)
                return carry

            lax.fori_loop(0, d // per_iter, class_group, 0)

    def combine(c, carry):
        rows = pl.ds(pl.multiple_of(c * SPAN, SPAN), SPAN)
        ms = [m_scr[g, rows, :] for g in range(len(DILATIONS))]
        m_all = functools.reduce(jnp.maximum, ms)
        num = jnp.zeros((SPAN, HEAD_DIM), F32)
        den = jnp.zeros((SPAN, HEAD_DIM), F32)
        for g in range(len(DILATIONS)):
            sc = jnp.exp(ms[g] - m_all)
            num = num + num_scr[g, rows, :] * sc
            den = den + l_scr[g, rows, :] * sc
        o_ref[rows, :] = (num / den).astype(o_ref.dtype)
        return carry

    lax.fori_loop(0, t // SPAN, combine, 0)


def _attn_prompt(qkv, n_heads):
    s = qkv.shape[0]
    t = ATTN_TILE
    assert s % t == 0, "prompt length must be a multiple of the dilation tile"
    cur = lambda off: pl.BlockSpec((t, HEAD_DIM), lambda h, i: (i, off + h))
    prev = lambda off: pl.BlockSpec((t, HEAD_DIM), lambda h, i: (jnp.maximum(i - 1, 0), off + h))
    return pl.pallas_call(
        _attn_prompt_kernel,
        grid=(n_heads, s // t),
        in_specs=[cur(0), prev(n_heads), cur(n_heads), prev(2 * n_heads), cur(2 * n_heads)],
        out_specs=pl.BlockSpec((t, HEAD_DIM), lambda h, i: (i, h)),
        out_shape=jax.ShapeDtypeStruct((s, n_heads * HEAD_DIM), BF16),
        scratch_shapes=[pltpu.VMEM((len(DILATIONS), t, HEAD_DIM), F32)] * 3,
        compiler_params=_params("parallel", "arbitrary"),
        name="attn_prompt",
    )(qkv, qkv, qkv, qkv, qkv)


def _attn_sample_kernel(qkv_ref, kt_ref, vt_ref, kf_ref, vf_ref, o_ref, kn_scr, vn_scr, *, n_heads):
    n_new = qkv_ref.shape[1]
    tail = kt_ref.shape[1]
    d_far = DILATIONS[-1]
    kn_scr[0:tail] = kt_ref[0]
    vn_scr[0:tail] = vt_ref[0]
    kn_scr[tail:tail + n_new] = qkv_ref[0, :, n_heads:2 * n_heads, :]
    vn_scr[tail:tail + n_new] = qkv_ref[0, :, 2 * n_heads:3 * n_heads, :]
    n_near_far = tail // d_far + 1

    for i in range(n_new):
        q = qkv_ref[0, i, 0:n_heads, :] * (HEAD_DIM ** -0.5)
        lists = []
        for d in DILATIONS[:-1]:
            rows = _rows(tail + i - d * SPAN, SPAN + 1, d)
            lists.append((kn_scr[rows], vn_scr[rows]))
        rows = _rows(tail + i - d_far * (n_near_far - 1), n_near_far, d_far)
        lists.append((kn_scr[rows], vn_scr[rows]))
        lists.append((kf_ref[0, :, i], vf_ref[0, :, i]))
        scores = [jnp.sum(k * q[None], axis=-1, keepdims=True) for k, _ in lists]
        m = functools.reduce(jnp.maximum, [jnp.max(s, axis=0) for s in scores])
        den = jnp.zeros((n_heads, 1), F32)
        num = jnp.zeros((n_heads, HEAD_DIM), F32)
        for s, (_, v) in zip(scores, lists):
            p = jnp.exp(s - m[None])
            den = den + jnp.sum(p, axis=0)
            num = num + jnp.sum(p * v, axis=0)
        o_ref[0, i] = num / den


def _attn_sample(qkv, cache_k, cache_v):
    b, n_new, _, _ = qkv.shape
    buf, n_heads = cache_k.shape[1], cache_k.shape[2]
    d_mid, d_far = DILATIONS[1], DILATIONS[2]
    tail = d_mid * SPAN
    assert buf == d_far * SPAN and n_new <= d_mid and tail % d_far == 0
    n_far = (buf - tail) // d_far
    far_shape = (b, buf // d_far, d_far, n_heads, HEAD_DIM)
    near = pl.BlockSpec((1, tail, n_heads, HEAD_DIM), lambda i: (i, buf // tail - 1, 0, 0))
    far = pl.BlockSpec((1, n_far, n_new, n_heads, HEAD_DIM), lambda i: (i, 0, 0, 0, 0))
    new = pl.BlockSpec((1, n_new, 3 * n_heads, HEAD_DIM), lambda i: (i, 0, 0, 0))
    ctx = pltpu.VMEM((tail + 8, n_heads, HEAD_DIM), F32)
    return pl.pallas_call(
        functools.partial(_attn_sample_kernel, n_heads=n_heads),
        grid=(b,),
        in_specs=[new, near, near, far, far],
        out_specs=pl.BlockSpec((1, n_new, n_heads, HEAD_DIM), lambda i: (i, 0, 0, 0)),
        out_shape=jax.ShapeDtypeStruct((b, n_new, n_heads, HEAD_DIM), F32),
        scratch_shapes=[ctx, ctx],
        compiler_params=_params("parallel"),
        name="attn_sample",
    )(qkv, cache_k, cache_v, cache_k.reshape(far_shape), cache_v.reshape(far_shape))


def _ln_swish(y, g, b):
    mu = jnp.mean(y, axis=-1, keepdims=True)
    yc = y - mu
    var = jnp.mean(yc * yc, axis=-1, keepdims=True)
    z = yc * lax.rsqrt(var + LN_EPS) * g + b
    return z * jax.nn.sigmoid(z)


def _conv_prompt_kernel(up_ref, u_ref, w_ref, b_ref, lg_ref, lb_ref, o_ref, ctx_scr, y_scr):
    tm, c = u_ref.shape
    n_slabs = c // LANES
    slab = lambda cs: slice(cs * LANES, (cs + 1) * LANES)

    @pl.when(pl.program_id(0) > 0)
    def _():
        for cs in range(n_slabs):
            ctx_scr[cs, 0:CONV_HALO, :] = up_ref[:, slab(cs)]

    @pl.when(pl.program_id(0) == 0)
    def _():
        for cs in range(n_slabs):
            ctx_scr[cs, 0:CONV_HALO, :] = jnp.zeros((CONV_HALO, LANES), F32)

    for cs in range(n_slabs):
        ctx_scr[cs, CONV_HALO:CONV_HALO + tm, :] = u_ref[:, slab(cs)]

    off = CONV_HALO - (CONV_K - 1)
    rc = 32

    for cs in range(n_slabs):
        taps = [jnp.broadcast_to(w_ref[j:j + 1, slab(cs)], (rc, LANES)) for j in range(CONV_K)]
        bias = jnp.broadcast_to(b_ref[:, slab(cs)], (rc, LANES))

        def chunk(i, carry, cs=cs, taps=taps, bias=bias):
            r0 = i * (2 * rc)
            for phase in range(2):
                acc = bias
                for j in range(CONV_K):
                    acc = acc + ctx_scr[cs, pl.ds(r0 + phase + off + j, rc, stride=2), :] * taps[j]
                y_scr[cs, pl.ds(r0 + phase, rc, stride=2), :] = acc
            return carry

        lax.fori_loop(0, tm // (2 * rc), chunk, 0)

    ln_rows = 16
    ln_unroll = 4

    def ln_rows_at(row0):
        rows = pl.ds(pl.multiple_of(row0, ln_rows), ln_rows)
        ys = [y_scr[cs, rows, :] for cs in range(n_slabs)]
        mu = jnp.sum(functools.reduce(jnp.add, ys), axis=-1, keepdims=True) * (1.0 / c)
        ycs = [y - mu for y in ys]
        var = jnp.sum(functools.reduce(jnp.add, [yc * yc for yc in ycs]), axis=-1, keepdims=True) * (1.0 / c)
        inv = lax.rsqrt(var + LN_EPS)
        for cs in range(n_slabs):
            z = ycs[cs] * inv * lg_ref[:, slab(cs)] + lb_ref[:, slab(cs)]
            o_ref[rows, slab(cs)] = (z * jax.nn.sigmoid(z)).astype(o_ref.dtype)

    def ln_chunk(i, carry):
        for j in range(ln_unroll):
            ln_rows_at((i * ln_unroll + j) * ln_rows)
        return carry

    lax.fori_loop(0, tm // (ln_rows * ln_unroll), ln_chunk, 0)


def _conv_prompt(u, conv_w, conv_b, ln_g, ln_b):
    s, c = u.shape
    tm = _tile(s, 256, CONV_HALO)
    vec = pl.BlockSpec((1, c), lambda i: (0, 0))
    return pl.pallas_call(
        _conv_prompt_kernel,
        grid=(s // tm,),
        in_specs=[pl.BlockSpec((CONV_HALO, c), lambda i: (jnp.maximum(i * (tm // CONV_HALO) - 1, 0), 0)),
                  pl.BlockSpec((tm, c), lambda i: (i, 0)),
                  pl.BlockSpec((CONV_K, c), lambda i: (0, 0)), vec, vec, vec],
        out_specs=pl.BlockSpec((tm, c), lambda i: (i, 0)),
        out_shape=jax.ShapeDtypeStruct((s, c), BF16),
        scratch_shapes=[pltpu.VMEM((c // LANES, CONV_HALO + tm, LANES), F32),
                        pltpu.VMEM((c // LANES, tm, LANES), F32)],
        compiler_params=_params("parallel"),
        name="conv_prompt",
    )(u, u, conv_w, conv_b, ln_g, ln_b)


def _conv_sample_kernel(st_ref, u_ref, w_ref, b_ref, lg_ref, lb_ref, o_ref, ns_ref, ctx_scr):
    n_state, n_new = st_ref.shape[1], u_ref.shape[1]
    ctx_scr[0:n_state, :] = st_ref[0]
    ctx_scr[n_state:n_state + n_new, :] = u_ref[0]
    acc = jnp.zeros(u_ref.shape[1:], F32)
    for j in range(CONV_K):
        acc = acc + ctx_scr[j:j + n_new, :] * w_ref[j:j + 1, :]
    o_ref[0] = _ln_swish(acc + b_ref[...], lg_ref[...], lb_ref[...])
    ns_ref[0] = ctx_scr[n_new:n_new + n_state, :]


def _conv_sample(state, u, conv_w, conv_b, ln_g, ln_b):
    b, n_state, c = state.shape
    n_new = u.shape[1]
    assert n_state == CONV_K - 1
    vec = pl.BlockSpec((1, c), lambda i: (0, 0))
    return pl.pallas_call(
        _conv_sample_kernel,
        grid=(b,),
        in_specs=[pl.BlockSpec((1, n_state, c), lambda i: (i, 0, 0)),
                  pl.BlockSpec((1, n_new, c), lambda i: (i, 0, 0)),
                  pl.BlockSpec((CONV_K, c), lambda i: (0, 0)), vec, vec, vec],
        out_specs=[pl.BlockSpec((1, n_new, c), lambda i: (i, 0, 0)),
                   pl.BlockSpec((1, n_state, c), lambda i: (i, 0, 0))],
        out_shape=[jax.ShapeDtypeStruct((b, n_new, c), F32),
                   jax.ShapeDtypeStruct((b, n_state, c), F32)],
        scratch_shapes=[pltpu.VMEM((n_state + n_new, c), F32)],
        compiler_params=_params("parallel"),
        name="conv_sample",
    )(state, u, conv_w, conv_b, ln_g, ln_b)


def _outproj_kernel(x_ref, a_ref, c_ref, wa_ref, wc_ref, o_ref, wa_bf, wc_bf):
    @pl.when(pl.program_id(1) == 0)
    def _():
        _cast_weight(wa_ref, wa_bf)
        _cast_weight(wc_ref, wc_bf)

    for r0, r1 in _row_parts(x_ref.shape[0]):
        acc = _dot(a_ref[r0:r1, :].astype(BF16), wa_bf[...]) + _dot(c_ref[r0:r1, :].astype(BF16), wc_bf[...])
        o_ref[r0:r1, :] = x_ref[r0:r1, :] + acc


def _outproj(x, attn, conv, w_out):
    m, d = x.shape
    a, c = attn.shape[1], conv.shape[1]
    assert a == c
    tm = _tile(m, 1024, 16)
    tn = _tile(d, 512, LANES)
    return pl.pallas_call(
        _outproj_kernel,
        grid=(d // tn, m // tm),
        in_specs=[pl.BlockSpec((tm, tn), lambda n, i: (i, n)),
                  pl.BlockSpec((tm, a), lambda n, i: (i, 0)),
                  pl.BlockSpec((tm, c), lambda n, i: (i, 0)),
                  pl.BlockSpec((a, tn), lambda n, i: (0, n)),
                  pl.BlockSpec((c, tn), lambda n, i: (1, n))],
        out_specs=pl.BlockSpec((tm, tn), lambda n, i: (i, n)),
        out_shape=jax.ShapeDtypeStruct((m, d), F32),
        scratch_shapes=[pltpu.VMEM((a, tn), BF16), pltpu.VMEM((c, tn), BF16)],
        compiler_params=_params("arbitrary", "arbitrary"),
        name="out_proj",
    )(x, attn, conv, w_out, w_out)


def _ffn_kernel(x_ref, g2_ref, wg_ref, wu_ref, wd_ref, gf_ref, *rest, roll):
    if roll is None:
        y_ref, h_ref = rest
    else:
        ck_ref, cv_ref, new_ref, y_ref, nk_ref, nv_ref, h_ref = rest
    f = pl.program_id(1)
    tm, d = y_ref.shape
    rc = _tile(tm, NORM_ROWS)

    def by_row_chunks(body):
        def step(c, carry):
            body(pl.ds(pl.multiple_of(c * rc, rc), rc))
            return carry
        lax.fori_loop(0, tm // rc, step, 0)

    @pl.when(f == 0)
    def _():
        def body(rows):
            x = x_ref[rows, :]
            h_ref[rows, :] = _rms_to_bf16(x, g2_ref[...])
            y_ref[rows, :] = x
        by_row_chunks(body)

    h = h_ref[...]
    gate = _dot(h, wg_ref[...])
    act = (gate * jax.nn.sigmoid(gate) * _dot(h, wu_ref[...])).astype(BF16)
    dc = _tile(d, FFN_DOWN_CHUNK, LANES)
    for c0 in range(0, d, dc):
        y_ref[:, c0:c0 + dc] += _dot(act, wd_ref[:, c0:c0 + dc])

    @pl.when(f == pl.num_programs(1) - 1)
    def _():
        def body(rows):
            y = y_ref[rows, :]
            ms = jnp.mean(y * y, axis=-1, keepdims=True)
            y_ref[rows, :] = y * lax.rsqrt(ms + RMS_EPS) * gf_ref[...]
        by_row_chunks(body)

    if roll is not None:
        step = pl.program_id(0) * pl.num_programs(1) + f
        _cache_roll_chunk(step, roll, ck_ref, cv_ref, new_ref, nk_ref, nv_ref)


def _cache_roll_chunk(step, roll, ck_ref, cv_ref, new_ref, nk_ref, nv_ref):
    n_chunks, per_seq, n_heads = roll
    n_new = new_ref.shape[1]
    r = nk_ref.shape[1]
    final_of_seq = (step < n_chunks) & (_roll_order(step, per_seq) == per_seq - 1)
    for i, (src, dst) in enumerate(((ck_ref, nk_ref), (cv_ref, nv_ref))):
        dst[...] = src[...]

        @pl.when(final_of_seq)
        def _(src=src, dst=dst, i=i):
            dst[0, 0:r - n_new] = src[0, n_new:r]
            dst[0, r - n_new:r] = new_ref[0, :, (i + 1) * n_heads:(i + 2) * n_heads, :]


def _roll_order(step, per_seq):
    return (step % per_seq + 1) % per_seq


def _ffn(x, g2, w_gate, w_up, w_down, gf, caches=None):
    m, d = x.shape
    dff = w_gate.shape[1]
    tm = _tile(m, 512)
    tf = _tile(dff, 256, LANES)
    nf = dff // tf
    grid = (m // tm, nf)
    vec = pl.BlockSpec((1, d), lambda i, f: (0, 0))
    in_specs = [pl.BlockSpec((tm, d), lambda i, f: (i, 0), pipeline_mode=pl.Buffered(1)), vec,
                pl.BlockSpec((d, tf), lambda i, f: (0, f)),
                pl.BlockSpec((d, tf), lambda i, f: (0, f)),
                pl.BlockSpec((tf, d), lambda i, f: (f, 0)), vec]
    out_specs = [pl.BlockSpec((tm, d), lambda i, f: (i, 0))]
    out_shape = [jax.ShapeDtypeStruct((m, d), F32)]
    args = [x, g2, w_gate, w_up, w_down, gf]
    roll = None
    if caches is not None:
        cache_k, cache_v, qkv_new = caches
        b, buf, n_heads, _ = cache_k.shape
        n_new = qkv_new.shape[1]
        assert buf == MAX_WINDOW, "the rolled cache keeps exactly the window"
        r = CACHE_ROLL_ROWS
        per_seq = buf // r
        n_chunks = b * per_seq
        assert buf % r == 0 and per_seq >= 2 and n_new < r and n_chunks <= grid[0] * grid[1]

        def chunk(i, f):
            c = jnp.minimum(i * nf + f, n_chunks - 1)
            return c // per_seq, _roll_order(c, per_seq)

        def src_map(i, f):
            seq, c = chunk(i, f)
            return seq, jnp.minimum(c * r + n_new, buf - r), 0, 0

        src = pl.BlockSpec((pl.Element(1), pl.Element(r), pl.Element(n_heads), pl.Element(HEAD_DIM)), src_map)
        dst = pl.BlockSpec((1, r, n_heads, HEAD_DIM), lambda i, f: (*chunk(i, f), 0, 0))
        new = pl.BlockSpec((1, n_new, 3 * n_heads, HEAD_DIM), lambda i, f: (chunk(i, f)[0], 0, 0, 0))
        in_specs += [src, src, new]
        out_specs += [dst, dst]
        out_shape += [jax.ShapeDtypeStruct((b, buf, n_heads, HEAD_DIM), F32)] * 2
        args += [cache_k, cache_v, qkv_new]
        roll = (n_chunks, per_seq, n_heads)
    out = pl.pallas_call(
        functools.partial(_ffn_kernel, roll=roll),
        grid=grid,
        in_specs=in_specs,
        out_specs=out_specs,
        out_shape=out_shape,
        scratch_shapes=[pltpu.VMEM((tm, d), BF16)],
        compiler_params=_params("arbitrary", "arbitrary"),
        name="ffn",
    )(*args)
    return out[0] if caches is None else out


def kernel(x_prompt, x_sample, cache_k, cache_v, state_conv, norm1_g, w_in, conv_w, conv_b, conv_ln_g,
           conv_ln_b, w_out, norm2_g, w_gate, w_up, w_down, final_g):
    depth = w_in.shape[0]
    bp, s, d = x_prompt.shape
    bs, t_new, _ = x_sample.shape
    assert depth == 1 and bp == 1, "one layer, one prompt sequence"
    attn_w = d // 2
    conv_c = d - attn_w
    n_heads = attn_w // HEAD_DIM
    buf = cache_k.shape[2]

    row = lambda v: v.reshape(1, -1)
    l = 0
    w_in_b, w_out_b = w_in[l], w_out[l]
    w_gate_b, w_up_b, w_down_b = w_gate[l].astype(BF16), w_up[l].astype(BF16), w_down[l].astype(BF16)
    g1, g2, gf = row(norm1_g[l]), row(norm2_g[l]), row(final_g)
    cb, lg, lb = row(conv_b[l]), row(conv_ln_g[l]), row(conv_ln_b[l])

    ms = bs * t_new
    xs = x_sample.reshape(ms, d)
    pos_s = PAST_LEN + jnp.tile(jnp.arange(t_new, dtype=jnp.int32), bs)
    qkv_s, u_s = _project_in(xs, g1, w_in_b, pos_s, attn_w, conv_c)
    qkv_s = qkv_s.reshape(bs, t_new, 3 * n_heads, HEAD_DIM)

    xp = x_prompt.reshape(s, d)
    qkv_p, u_p = _project_in(xp, g1, w_in_b, jnp.arange(s, dtype=jnp.int32), attn_w, conv_c)
    attn_p = _attn_prompt(qkv_p, n_heads)
    conv_p = _conv_prompt(u_p, conv_w[l], cb, lg, lb)
    x1_p = _outproj(xp, attn_p, conv_p, w_out_b)
    y_p, nk_s, nv_s = _ffn(x1_p, g2, w_gate_b, w_up_b, w_down_b, gf, caches=(cache_k[l], cache_v[l], qkv_s))
    keep_p = min(MAX_WINDOW, s)
    nk_p = qkv_p[s - keep_p:, attn_w:2 * attn_w].reshape(1, 1, keep_p, n_heads, HEAD_DIM)
    nv_p = qkv_p[s - keep_p:, 2 * attn_w:].reshape(1, 1, keep_p, n_heads, HEAD_DIM)
    n_state = CONV_K - 1
    assert s >= n_state
    nc_p = u_p[s - n_state:].reshape(1, 1, n_state, conv_c)

    attn_s = _attn_sample(qkv_s, cache_k[l], cache_v[l])
    conv_s, nc_s = _conv_sample(state_conv[l], u_s.reshape(bs, t_new, conv_c), conv_w[l], cb, lg, lb)
    x1_s = _outproj(xs, attn_s.reshape(ms, attn_w), conv_s.reshape(ms, conv_c), w_out_b)
    y_s = _ffn(x1_s, g2, w_gate_b, w_up_b, w_down_b, gf)

    return (y_p.reshape(1, s, d), y_s.reshape(bs, t_new, d), nk_p, nv_p, nc_p,
            nk_s[None], nv_s[None], nc_s.reshape(1, bs, n_state, conv_c))
```

```python
import functools
import math

import jax
import jax.numpy as jnp
from jax import lax
from jax.experimental import pallas as pl
from jax.experimental.pallas import tpu as pltpu

F32 = jnp.float32
BF16 = jnp.bfloat16

HEAD_DIM = 128
ROPE_DIM = HEAD_DIM // 4
ROPE_THETA = 500000.0
CONV_K = 31
RMS_EPS = 1e-6
LN_EPS = 1e-5
PAST_LEN = 8192
DILATIONS = (1, 4, 16)
SPAN = 128
MAX_WINDOW = 2048
ATTN_TILE = MAX_WINDOW
ATTN_UNROLL = 8
CONV_HALO = 32
FFN_DOWN_CHUNK = 1024
NORM_ROWS = 64
WEIGHT_CAST_ROWS = 512
CACHE_ROLL_ROWS = 128

LANES = 128
VMEM_LIMIT_BYTES = 56 * 1024 * 1024


def _params(*sem):
    return pltpu.CompilerParams(dimension_semantics=sem, vmem_limit_bytes=VMEM_LIMIT_BYTES)


def _tile(n, pref, mult=8):
    if n <= pref:
        return n
    for t in range(pref - pref % mult, 0, -mult):
        if n % t == 0:
            return t
    raise ValueError(f"no tile for {n}")


def _rms_to_bf16(x, g):
    ms = jnp.mean(x * x, axis=-1, keepdims=True)
    return (x * lax.rsqrt(ms + RMS_EPS) * g).astype(BF16)


def _dot(a, b):
    return jnp.dot(a, b, preferred_element_type=F32)


def _dot_nt(a, b):
    return lax.dot_general(a, b, (((1,), (1,)), ((), ())), preferred_element_type=F32)


def _norm_kernel(x_ref, g_ref, h_ref):
    h_ref[...] = _rms_to_bf16(x_ref[...], g_ref[...])


def _cast_weight(w_ref, w_bf):
    k = w_ref.shape[0]
    rc = _tile(k, WEIGHT_CAST_ROWS)
    for r0 in range(0, k, rc):
        w_bf[r0:r0 + rc, :] = w_ref[r0:r0 + rc, :].astype(BF16)


def _row_parts(tm):
    half = tm // 2
    return ((0, tm),) if half % 16 else ((0, half), (half, tm))


def _qkv_kernel(h_ref, w_ref, tab_ref, o_ref, w_bf):
    @pl.when(pl.program_id(1) == 0)
    def _():
        _cast_weight(w_ref, w_bf)

    for r0, r1 in _row_parts(h_ref.shape[0]):
        z = _dot(h_ref[r0:r1, :], w_bf[...])
        c, s1, s2 = tab_ref[0, 0, r0:r1, :], tab_ref[0, 1, r0:r1, :], tab_ref[0, 2, r0:r1, :]
        for j in range(z.shape[1] // HEAD_DIM):
            zj = z[:, j * HEAD_DIM:(j + 1) * HEAD_DIM]
            lo = pltpu.roll(zj, ROPE_DIM // 2, 1)
            hi = pltpu.roll(zj, HEAD_DIM - ROPE_DIM // 2, 1)
            o_ref[r0:r1, j * HEAD_DIM:(j + 1) * HEAD_DIM] = zj * c + lo * s1 + hi * s2


def _glu_kernel(h_ref, wa_ref, wg_ref, u_ref, wa_bf, wg_bf):
    @pl.when(pl.program_id(1) == 0)
    def _():
        _cast_weight(wa_ref, wa_bf)
        _cast_weight(wg_ref, wg_bf)

    for r0, r1 in _row_parts(h_ref.shape[0]):
        h = h_ref[r0:r1, :]
        u_ref[r0:r1, :] = _dot(h, wa_bf[...]) * jax.nn.sigmoid(_dot(h, wg_bf[...]))


def _rope_tables(pos):
    half = ROPE_DIM // 2
    inv = ROPE_THETA ** (-jnp.arange(half, dtype=F32) / half)
    ang = pos.astype(F32)[:, None] * inv[None, :]
    cos, sin = jnp.cos(ang), jnp.sin(ang)
    m = pos.shape[0]
    one = jnp.ones((m, HEAD_DIM - ROPE_DIM), F32)
    zero = jnp.zeros((m, HEAD_DIM - ROPE_DIM), F32)
    zh = jnp.zeros((m, half), F32)
    c = jnp.concatenate([cos, cos, one], axis=1)
    s1 = jnp.concatenate([zh, sin, zero], axis=1)
    s2 = jnp.concatenate([-sin, zh, zero], axis=1)
    rot = jnp.stack([c, s1, s2])
    ident = jnp.stack([jnp.ones_like(c), jnp.zeros_like(c), jnp.zeros_like(c)])
    return jnp.stack([rot, ident])


def _project_in(x, g1, w_in, pos, attn_w, conv_c):
    m, d = x.shape
    tr = _tile(m, 256)
    h = pl.pallas_call(
        _norm_kernel,
        grid=(m // tr,),
        in_specs=[pl.BlockSpec((tr, d), lambda i: (i, 0)), pl.BlockSpec((1, d), lambda i: (0, 0))],
        out_specs=pl.BlockSpec((tr, d), lambda i: (i, 0)),
        out_shape=jax.ShapeDtypeStruct((m, d), BF16),
        compiler_params=_params("parallel"),
        name="norm1",
    )(x, g1)

    tm = _tile(m, 1024, 16)
    tn = _tile(attn_w, 1024, LANES)
    once = pl.Buffered(1) if m // tm > 1 else None
    n_rope = 2 * attn_w // tn
    qkv = pl.pallas_call(
        _qkv_kernel,
        grid=(3 * attn_w // tn, m // tm),
        in_specs=[pl.BlockSpec((tm, d), lambda n, i: (i, 0)),
                  pl.BlockSpec((d, tn), lambda n, i: (0, n), pipeline_mode=once),
                  pl.BlockSpec((1, 3, tm, HEAD_DIM), lambda n, i: (jnp.where(n < n_rope, 0, 1), 0, i, 0))],
        out_specs=pl.BlockSpec((tm, tn), lambda n, i: (i, n)),
        out_shape=jax.ShapeDtypeStruct((m, 3 * attn_w), F32),
        scratch_shapes=[pltpu.VMEM((d, tn), BF16)],
        compiler_params=_params("arbitrary", "arbitrary"),
        name="qkv_proj",
    )(h, w_in, _rope_tables(pos))

    tc = _tile(conv_c, 512, LANES)
    a0 = 3 * attn_w // tc
    g0 = (3 * attn_w + conv_c) // tc
    u = pl.pallas_call(
        _glu_kernel,
        grid=(conv_c // tc, m // tm),
        in_specs=[pl.BlockSpec((tm, d), lambda n, i: (i, 0)),
                  pl.BlockSpec((d, tc), lambda n, i: (0, a0 + n), pipeline_mode=once),
                  pl.BlockSpec((d, tc), lambda n, i: (0, g0 + n), pipeline_mode=once)],
        out_specs=pl.BlockSpec((tm, tc), lambda n, i: (i, n)),
        out_shape=jax.ShapeDtypeStruct((m, conv_c), F32),
        scratch_shapes=[pltpu.VMEM((d, tc), BF16)] * 2,
        compiler_params=_params("arbitrary", "arbitrary"),
        name="glu_proj",
    )(h, w_in, w_in)
    return qkv, u


def _rows(start, size, stride):
    return pl.ds(start, size) if stride == 1 else pl.ds(start, size, stride=stride)


def _attn_prompt_kernel(q_ref, kp_ref, kc_ref, vp_ref, vc_ref, *rest):
    n_w = (len(rest) - 4) // 2
    w_in, o_ref, w_out = rest[:n_w], rest[n_w], rest[n_w + 1:2 * n_w + 1]
    num_scr, m_scr, l_scr = rest[2 * n_w + 1:]
    for src, dst in zip(w_in, w_out):
        rc = _tile(src.shape[0], 16, 16)
        for r0 in range(0, src.shape[0], rc):
            dst[r0:r0 + rc, :] = src[r0:r0 + rc, :].astype(BF16)

    tile = pl.program_id(1)
    t = q_ref.shape[0]
    scale = HEAD_DIM ** -0.5
    qi = lax.broadcasted_iota(jnp.int32, (SPAN, 2 * SPAN), 0)
    kj = lax.broadcasted_iota(jnp.int32, (SPAN, 2 * SPAN), 1)
    band = (kj >= qi) & (kj <= qi + SPAN)
    first_key = jnp.where(tile > 0, 0, SPAN)
    band_first = band & (kj >= first_key)

    def attend(gi, d, q_start, q, k, v, mask):
        s = _dot_nt(q.astype(BF16), k.astype(BF16)) * scale
        s = jnp.where(mask, s, -jnp.inf)
        m = jnp.max(s, axis=-1, keepdims=True)
        p = jnp.exp(s - m)
        l = jnp.sum(p, axis=-1, keepdims=True)
        num = _dot(p.astype(BF16), v.astype(BF16))
        rows = _rows(q_start, SPAN, d)
        num_scr[gi, rows, :] = num
        m_scr[gi, rows, :] = jnp.broadcast_to(m, (SPAN, HEAD_DIM))
        l_scr[gi, rows, :] = jnp.broadcast_to(l, (SPAN, HEAD_DIM))

    def first_block(gi, d, r):
        q = q_ref[_rows(r, SPAN, d), :]
        k = jnp.concatenate([kp_ref[_rows(t - d * SPAN + r, SPAN, d), :],
                             kc_ref[_rows(r, SPAN, d), :]], axis=0)
        v = jnp.concatenate([vp_ref[_rows(t - d * SPAN + r, SPAN, d), :],
                             vc_ref[_rows(r, SPAN, d), :]], axis=0)
        attend(gi, d, r, q, k, v, band_first)

    def later_block(gi, d, r, b):
        q_start = r + d * SPAN * b
        if d == 1:
            q_start = pl.multiple_of(q_start, SPAN)
        k_rows = _rows(q_start - d * SPAN, 2 * SPAN, d)
        attend(gi, d, q_start, q_ref[_rows(q_start, SPAN, d), :], kc_ref[k_rows, :], vc_ref[k_rows, :], band)

    u = ATTN_UNROLL
    for gi, d in enumerate(DILATIONS):
        n_blocks = t // (d * SPAN)
        if n_blocks >= u:
            assert n_blocks % u == 0

            def class_body(r, carry, gi=gi, d=d, n_blocks=n_blocks):
                first_block(gi, d, r)
                for b in range(1, u):
                    later_block(gi, d, r, b)

                def rest(i, c):
                    for j in range(u):
                        later_block(gi, d, r, i * u + j)
                    return c

                if n_blocks > u:
                    lax.fori_loop(1, n_blocks // u, rest, 0)
                return carry

            if d == 1:
                class_body(0, 0)
            else:
                lax.fori_loop(0, d, class_body, 0)
        else:
            per_iter = u // n_blocks
            assert u % n_blocks == 0 and d % per_iter == 0

            def class_group(i, carry, gi=gi, d=d, n_blocks=n_blocks, per_iter=per_iter):
                for j in range(per_iter):
                    r = i * per_iter + j
                    first_block(gi, d, r)
                    for b in range(1, n_blocks):
                        later_block(gi, d, r, b)
                return carry

            lax.fori_loop(0, d // per_iter, class_group, 0)

    def combine(c, carry):
        rows = pl.ds(pl.multiple_of(c * SPAN, SPAN), SPAN)
        ms = [m_scr[g, rows, :] for g in range(len(DILATIONS))]
        m_all = functools.reduce(jnp.maximum, ms)
        num = jnp.zeros((SPAN, HEAD_DIM), F32)
        den = jnp.zeros((SPAN, HEAD_DIM), F32)
        for g in range(len(DILATIONS)):
            sc = jnp.exp(ms[g] - m_all)
            num = num + num_scr[g, rows, :] * sc
            den = den + l_scr[g, rows, :] * sc
        o_ref[rows, :] = (num / den).astype(o_ref.dtype)
        return carry

    lax.fori_loop(0, t // SPAN, combine, 0)


def _attn_prompt(qkv, n_heads, weights=()):
    s = qkv.shape[0]
    t = ATTN_TILE
    assert s % t == 0, "prompt length must be a multiple of the dilation tile"
    n_tiles = s // t
    cur = lambda off: pl.BlockSpec((t, HEAD_DIM), lambda h, i: (i, off + h))
    prev = lambda off: pl.BlockSpec((t, HEAD_DIM), lambda h, i: (jnp.maximum(i - 1, 0), off + h))
    w_specs, w_shapes = [], []
    for w in weights:
        rows, cols = w.shape
        rb = next(r for r in range(16, rows + 1, 16) if rows % r == 0 and rows // r <= n_heads * n_tiles)
        spec = pl.BlockSpec((rb, cols), lambda h, i, nb=rows // rb: (jnp.minimum(h * n_tiles + i, nb - 1), 0))
        w_specs.append(spec)
        w_shapes.append(jax.ShapeDtypeStruct(w.shape, BF16))
    out = pl.pallas_call(
        _attn_prompt_kernel,
        grid=(n_heads, n_tiles),
        in_specs=[cur(0), prev(n_heads), cur(n_heads), prev(2 * n_heads), cur(2 * n_heads)] + w_specs,
        out_specs=[pl.BlockSpec((t, HEAD_DIM), lambda h, i: (i, h))] + w_specs,
        out_shape=[jax.ShapeDtypeStruct((s, n_heads * HEAD_DIM), BF16)] + w_shapes,
        scratch_shapes=[pltpu.VMEM((len(DILATIONS), t, HEAD_DIM), F32)] * 3,
        compiler_params=_params("arbitrary", "arbitrary"),
        name="attn_prompt",
    )(qkv, qkv, qkv, qkv, qkv, *weights)
    return out[0], tuple(out[1:])


def _attn_sample_kernel(qkv_ref, kt_ref, vt_ref, kf_ref, vf_ref, o_ref, kn_scr, vn_scr, *, n_heads):
    n_new = qkv_ref.shape[1]
    tail = kt_ref.shape[1]
    d_far = DILATIONS[-1]
    kn_scr[0:tail] = kt_ref[0]
    vn_scr[0:tail] = vt_ref[0]
    kn_scr[tail:tail + n_new] = qkv_ref[0, :, n_heads:2 * n_heads, :]
    vn_scr[tail:tail + n_new] = qkv_ref[0, :, 2 * n_heads:3 * n_heads, :]
    n_near_far = tail // d_far + 1

    for i in range(n_new):
        q = qkv_ref[0, i, 0:n_heads, :] * (HEAD_DIM ** -0.5)
        lists = []
        for d in DILATIONS[:-1]:
            rows = _rows(tail + i - d * SPAN, SPAN + 1, d)
            lists.append((kn_scr[rows], vn_scr[rows]))
        rows = _rows(tail + i - d_far * (n_near_far - 1), n_near_far, d_far)
        lists.append((kn_scr[rows], vn_scr[rows]))
        lists.append((kf_ref[0, :, i], vf_ref[0, :, i]))
        scores = [jnp.sum(k * q[None], axis=-1, keepdims=True) for k, _ in lists]
        m = functools.reduce(jnp.maximum, [jnp.max(s, axis=0) for s in scores])
        den = jnp.zeros((n_heads, 1), F32)
        num = jnp.zeros((n_heads, HEAD_DIM), F32)
        for s, (_, v) in zip(scores, lists):
            p = jnp.exp(s - m[None])
            den = den + jnp.sum(p, axis=0)
            num = num + jnp.sum(p * v, axis=0)
        o_ref[0, i] = num / den


def _attn_sample(qkv, cache_k, cache_v):
    b, n_new, _, _ = qkv.shape
    buf, n_heads = cache_k.shape[1], cache_k.shape[2]
    d_mid, d_far = DILATIONS[1], DILATIONS[2]
    tail = d_mid * SPAN
    assert buf == d_far * SPAN and n_new <= d_mid and tail % d_far == 0
    n_far = (buf - tail) // d_far
    far_shape = (b, buf // d_far, d_far, n_heads, HEAD_DIM)
    near = pl.BlockSpec((1, tail, n_heads, HEAD_DIM), lambda i: (i, buf // tail - 1, 0, 0))
    far = pl.BlockSpec((1, n_far, n_new, n_heads, HEAD_DIM), lambda i: (i, 0, 0, 0, 0))
    new = pl.BlockSpec((1, n_new, 3 * n_heads, HEAD_DIM), lambda i: (i, 0, 0, 0))
    ctx = pltpu.VMEM((tail + 8, n_heads, HEAD_DIM), F32)
    return pl.pallas_call(
        functools.partial(_attn_sample_kernel, n_heads=n_heads),
        grid=(b,),
        in_specs=[new, near, near, far, far],
        out_specs=pl.BlockSpec((1, n_new, n_heads, HEAD_DIM), lambda i: (i, 0, 0, 0)),
        out_shape=jax.ShapeDtypeStruct((b, n_new, n_heads, HEAD_DIM), F32),
        scratch_shapes=[ctx, ctx],
        compiler_params=_params("parallel"),
        name="attn_sample",
    )(qkv, cache_k, cache_v, cache_k.reshape(far_shape), cache_v.reshape(far_shape))


def _ln_swish(y, g, b):
    mu = jnp.mean(y, axis=-1, keepdims=True)
    yc = y - mu
    var = jnp.mean(yc * yc, axis=-1, keepdims=True)
    z = yc * lax.rsqrt(var + LN_EPS) * g + b
    return z * jax.nn.sigmoid(z)


def _conv_prompt_kernel(up_ref, u_ref, w_ref, b_ref, lg_ref, lb_ref, o_ref, ctx_scr, y_scr):
    tm, c = u_ref.shape
    n_slabs = c // LANES
    slab = lambda cs: slice(cs * LANES, (cs + 1) * LANES)

    @pl.when(pl.program_id(0) > 0)
    def _():
        for cs in range(n_slabs):
            ctx_scr[cs, 0:CONV_HALO, :] = up_ref[:, slab(cs)]

    @pl.when(pl.program_id(0) == 0)
    def _():
        for cs in range(n_slabs):
            ctx_scr[cs, 0:CONV_HALO, :] = jnp.zeros((CONV_HALO, LANES), F32)

    for cs in range(n_slabs):
        ctx_scr[cs, CONV_HALO:CONV_HALO + tm, :] = u_ref[:, slab(cs)]

    off = CONV_HALO - (CONV_K - 1)
    rc = 32

    for cs in range(n_slabs):
        taps = [jnp.broadcast_to(w_ref[j:j + 1, slab(cs)], (rc, LANES)) for j in range(CONV_K)]
        bias = jnp.broadcast_to(b_ref[:, slab(cs)], (rc, LANES))

        def chunk(i, carry, cs=cs, taps=taps, bias=bias):
            r0 = i * (2 * rc)
            for phase in range(2):
                acc = bias
                for j in range(CONV_K):
                    acc = acc + ctx_scr[cs, pl.ds(r0 + phase + off + j, rc, stride=2), :] * taps[j]
                y_scr[cs, pl.ds(r0 + phase, rc, stride=2), :] = acc
            return carry

        lax.fori_loop(0, tm // (2 * rc), chunk, 0)

    ln_rows = 16
    ln_unroll = 4

    def ln_rows_at(row0):
        rows = pl.ds(pl.multiple_of(row0, ln_rows), ln_rows)
        ys = [y_scr[cs, rows, :] for cs in range(n_slabs)]
        mu = jnp.sum(functools.reduce(jnp.add, ys), axis=-1, keepdims=True) * (1.0 / c)
        ycs = [y - mu for y in ys]
        var = jnp.sum(functools.reduce(jnp.add, [yc * yc for yc in ycs]), axis=-1, keepdims=True) * (1.0 / c)
        inv = lax.rsqrt(var + LN_EPS)
        for cs in range(n_slabs):
            z = ycs[cs] * inv * lg_ref[:, slab(cs)] + lb_ref[:, slab(cs)]
            o_ref[rows, slab(cs)] = (z * jax.nn.sigmoid(z)).astype(o_ref.dtype)

    def ln_chunk(i, carry):
        for j in range(ln_unroll):
            ln_rows_at((i * ln_unroll + j) * ln_rows)
        return carry

    lax.fori_loop(0, tm // (ln_rows * ln_unroll), ln_chunk, 0)


def _conv_prompt(u, conv_w, conv_b, ln_g, ln_b):
    s, c = u.shape
    tm = _tile(s, 256, CONV_HALO)
    vec = pl.BlockSpec((1, c), lambda i: (0, 0))
    return pl.pallas_call(
        _conv_prompt_kernel,
        grid=(s // tm,),
        in_specs=[pl.BlockSpec((CONV_HALO, c), lambda i: (jnp.maximum(i * (tm // CONV_HALO) - 1, 0), 0)),
                  pl.BlockSpec((tm, c), lambda i: (i, 0)),
                  pl.BlockSpec((CONV_K, c), lambda i: (0, 0)), vec, vec, vec],
        out_specs=pl.BlockSpec((tm, c), lambda i: (i, 0)),
        out_shape=jax.ShapeDtypeStruct((s, c), BF16),
        scratch_shapes=[pltpu.VMEM((c // LANES, CONV_HALO + tm, LANES), F32),
                        pltpu.VMEM((c // LANES, tm, LANES), F32)],
        compiler_params=_params("parallel"),
        name="conv_prompt",
    )(u, u, conv_w, conv_b, ln_g, ln_b)


def _conv_sample_kernel(st_ref, u_ref, w_ref, b_ref, lg_ref, lb_ref, o_ref, ns_ref, ctx_scr):
    n_state, n_new = st_ref.shape[1], u_ref.shape[1]
    ctx_scr[0:n_state, :] = st_ref[0]
    ctx_scr[n_state:n_state + n_new, :] = u_ref[0]
    acc = jnp.zeros(u_ref.shape[1:], F32)
    for j in range(CONV_K):
        acc = acc + ctx_scr[j:j + n_new, :] * w_ref[j:j + 1, :]
    o_ref[0] = _ln_swish(acc + b_ref[...], lg_ref[...], lb_ref[...])
    ns_ref[0] = ctx_scr[n_new:n_new + n_state, :]


def _conv_sample(state, u, conv_w, conv_b, ln_g, ln_b):
    b, n_state, c = state.shape
    n_new = u.shape[1]
    assert n_state == CONV_K - 1
    vec = pl.BlockSpec((1, c), lambda i: (0, 0))
    return pl.pallas_call(
        _conv_sample_kernel,
        grid=(b,),
        in_specs=[pl.BlockSpec((1, n_state, c), lambda i: (i, 0, 0)),
                  pl.BlockSpec((1, n_new, c), lambda i: (i, 0, 0)),
                  pl.BlockSpec((CONV_K, c), lambda i: (0, 0)), vec, vec, vec],
        out_specs=[pl.BlockSpec((1, n_new, c), lambda i: (i, 0, 0)),
                   pl.BlockSpec((1, n_state, c), lambda i: (i, 0, 0))],
        out_shape=[jax.ShapeDtypeStruct((b, n_new, c), F32),
                   jax.ShapeDtypeStruct((b, n_state, c), F32)],
        scratch_shapes=[pltpu.VMEM((n_state + n_new, c), F32)],
        compiler_params=_params("parallel"),
        name="conv_sample",
    )(state, u, conv_w, conv_b, ln_g, ln_b)


def _outproj_kernel(x_ref, a_ref, c_ref, wa_ref, wc_ref, o_ref, wa_bf, wc_bf):
    @pl.when(pl.program_id(1) == 0)
    def _():
        _cast_weight(wa_ref, wa_bf)
        _cast_weight(wc_ref, wc_bf)

    for r0, r1 in _row_parts(x_ref.shape[0]):
        acc = _dot(a_ref[r0:r1, :].astype(BF16), wa_bf[...]) + _dot(c_ref[r0:r1, :].astype(BF16), wc_bf[...])
        o_ref[r0:r1, :] = x_ref[r0:r1, :] + acc


def _outproj(x, attn, conv, w_out):
    m, d = x.shape
    a, c = attn.shape[1], conv.shape[1]
    assert a == c
    tm = _tile(m, 1024, 16)
    tn = _tile(d, 512, LANES)
    return pl.pallas_call(
        _outproj_kernel,
        grid=(d // tn, m // tm),
        in_specs=[pl.BlockSpec((tm, tn), lambda n, i: (i, n)),
                  pl.BlockSpec((tm, a), lambda n, i: (i, 0)),
                  pl.BlockSpec((tm, c), lambda n, i: (i, 0)),
                  pl.BlockSpec((a, tn), lambda n, i: (0, n)),
                  pl.BlockSpec((c, tn), lambda n, i: (1, n))],
        out_specs=pl.BlockSpec((tm, tn), lambda n, i: (i, n)),
        out_shape=jax.ShapeDtypeStruct((m, d), F32),
        scratch_shapes=[pltpu.VMEM((a, tn), BF16), pltpu.VMEM((c, tn), BF16)],
        compiler_params=_params("arbitrary", "arbitrary"),
        name="out_proj",
    )(x, attn, conv, w_out, w_out)


def _ffn_kernel(x_ref, g2_ref, wg_ref, wu_ref, wd_ref, gf_ref, *rest, roll):
    if roll is None:
        y_ref, h_ref = rest
    else:
        ck_ref, cv_ref, new_ref, y_ref, nk_ref, nv_ref, h_ref = rest
    f = pl.program_id(1)
    tm, d = y_ref.shape
    rc = _tile(tm, NORM_ROWS)

    def by_row_chunks(body):
        def step(c, carry):
            body(pl.ds(pl.multiple_of(c * rc, rc), rc))
            return carry
        lax.fori_loop(0, tm // rc, step, 0)

    @pl.when(f == 0)
    def _():
        def body(rows):
            x = x_ref[rows, :]
            h_ref[rows, :] = _rms_to_bf16(x, g2_ref[...])
            y_ref[rows, :] = x
        by_row_chunks(body)

    h = h_ref[...]
    gate = _dot(h, wg_ref[...])
    act = (gate * jax.nn.sigmoid(gate) * _dot(h, wu_ref[...])).astype(BF16)
    dc = _tile(d, FFN_DOWN_CHUNK, LANES)
    for c0 in range(0, d, dc):
        y_ref[:, c0:c0 + dc] += _dot(act, wd_ref[:, c0:c0 + dc])

    @pl.when(f == pl.num_programs(1) - 1)
    def _():
        def body(rows):
            y = y_ref[rows, :]
            ms = jnp.mean(y * y, axis=-1, keepdims=True)
            y_ref[rows, :] = y * lax.rsqrt(ms + RMS_EPS) * gf_ref[...]
        by_row_chunks(body)

    if roll is not None:
        step = pl.program_id(0) * pl.num_programs(1) + f
        _cache_roll_chunk(step, roll, ck_ref, cv_ref, new_ref, nk_ref, nv_ref)


def _cache_roll_chunk(step, roll, ck_ref, cv_ref, new_ref, nk_ref, nv_ref):
    n_chunks, per_seq, n_heads = roll
    n_new = new_ref.shape[1]
    r = nk_ref.shape[1]
    final_of_seq = (step < n_chunks) & (_roll_order(step, per_seq) == per_seq - 1)
    for i, (src, dst) in enumerate(((ck_ref, nk_ref), (cv_ref, nv_ref))):
        dst[...] = src[...]

        @pl.when(final_of_seq)
        def _(src=src, dst=dst, i=i):
            dst[0, 0:r - n_new] = src[0, n_new:r]
            dst[0, r - n_new:r] = new_ref[0, :, (i + 1) * n_heads:(i + 2) * n_heads, :]


def _roll_order(step, per_seq):
    return (step % per_seq + 1) % per_seq


def _ffn(x, g2, w_gate, w_up, w_down, gf, caches=None):
    m, d = x.shape
    dff = w_gate.shape[1]
    tm = _tile(m, 512)
    tf = _tile(dff, 256, LANES)
    nf = dff // tf
    grid = (m // tm, nf)
    vec = pl.BlockSpec((1, d), lambda i, f: (0, 0))
    in_specs = [pl.BlockSpec((tm, d), lambda i, f: (i, 0), pipeline_mode=pl.Buffered(1)), vec,
                pl.BlockSpec((d, tf), lambda i, f: (0, f)),
                pl.BlockSpec((d, tf), lambda i, f: (0, f)),
                pl.BlockSpec((tf, d), lambda i, f: (f, 0)), vec]
    out_specs = [pl.BlockSpec((tm, d), lambda i, f: (i, 0))]
    out_shape = [jax.ShapeDtypeStruct((m, d), F32)]
    args = [x, g2, w_gate, w_up, w_down, gf]
    roll = None
    if caches is not None:
        cache_k, cache_v, qkv_new = caches
        b, buf, n_heads, _ = cache_k.shape
        n_new = qkv_new.shape[1]
        assert buf == MAX_WINDOW, "the rolled cache keeps exactly the window"
        r = CACHE_ROLL_ROWS
        per_seq = buf // r
        n_chunks = b * per_seq
        assert buf % r == 0 and per_seq >= 2 and n_new < r and n_chunks <= grid[0] * grid[1]

        def chunk(i, f):
            c = jnp.minimum(i * nf + f, n_chunks - 1)
            return c // per_seq, _roll_order(c, per_seq)

        def src_map(i, f):
            seq, c = chunk(i, f)
            return seq, jnp.minimum(c * r + n_new, buf - r), 0, 0

        src = pl.BlockSpec((pl.Element(1), pl.Element(r), pl.Element(n_heads), pl.Element(HEAD_DIM)), src_map)
        dst = pl.BlockSpec((1, r, n_heads, HEAD_DIM), lambda i, f: (*chunk(i, f), 0, 0))
        new = pl.BlockSpec((1, n_new, 3 * n_heads, HEAD_DIM), lambda i, f: (chunk(i, f)[0], 0, 0, 0))
        in_specs += [src, src, new]
        out_specs += [dst, dst]
        out_shape += [jax.ShapeDtypeStruct((b, buf, n_heads, HEAD_DIM), F32)] * 2
        args += [cache_k, cache_v, qkv_new]
        roll = (n_chunks, per_seq, n_heads)
    out = pl.pallas_call(
        functools.partial(_ffn_kernel, roll=roll),
        grid=grid,
        in_specs=in_specs,
        out_specs=out_specs,
        out_shape=out_shape,
        scratch_shapes=[pltpu.VMEM((tm, d), BF16)],
        compiler_params=_params("arbitrary", "arbitrary"),
        name="ffn",
    )(*args)
    return out[0] if caches is None else out


def kernel(x_prompt, x_sample, cache_k, cache_v, state_conv, norm1_g, w_in, conv_w, conv_b, conv_ln_g,
           conv_ln_b, w_out, norm2_g, w_gate, w_up, w_down, final_g):
    depth = w_in.shape[0]
    bp, s, d = x_prompt.shape
    bs, t_new, _ = x_sample.shape
    assert depth == 1 and bp == 1, "one layer, one prompt sequence"
    attn_w = d // 2
    conv_c = d - attn_w
    n_heads = attn_w // HEAD_DIM
    buf = cache_k.shape[2]

    row = lambda v: v.reshape(1, -1)
    l = 0
    w_in_b, w_out_b = w_in[l], w_out[l]
    g1, g2, gf = row(norm1_g[l]), row(norm2_g[l]), row(final_g)
    cb, lg, lb = row(conv_b[l]), row(conv_ln_g[l]), row(conv_ln_b[l])

    ms = bs * t_new
    xs = x_sample.reshape(ms, d)
    pos_s = PAST_LEN + jnp.tile(jnp.arange(t_new, dtype=jnp.int32), bs)
    qkv_s, u_s = _project_in(xs, g1, w_in_b, pos_s, attn_w, conv_c)
    qkv_s = qkv_s.reshape(bs, t_new, 3 * n_heads, HEAD_DIM)

    xp = x_prompt.reshape(s, d)
    qkv_p, u_p = _project_in(xp, g1, w_in_b, jnp.arange(s, dtype=jnp.int32), attn_w, conv_c)
    attn_p, (w_gate_b, w_up_b, w_down_b) = _attn_prompt(qkv_p, n_heads, (w_gate[l], w_up[l], w_down[l]))
    conv_p = _conv_prompt(u_p, conv_w[l], cb, lg, lb)
    x1_p = _outproj(xp, attn_p, conv_p, w_out_b)
    y_p, nk_s, nv_s = _ffn(x1_p, g2, w_gate_b, w_up_b, w_down_b, gf, caches=(cache_k[l], cache_v[l], qkv_s))
    keep_p = min(MAX_WINDOW, s)
    nk_p = qkv_p[s - keep_p:, attn_w:2 * attn_w].reshape(1, 1, keep_p, n_heads, HEAD_DIM)
    nv_p = qkv_p[s - keep_p:, 2 * attn_w:].reshape(1, 1, keep_p, n_heads, HEAD_DIM)
    n_state = CONV_K - 1
    assert s >= n_state
    nc_p = u_p[s - n_state:].reshape(1, 1, n_state, conv_c)

    attn_s = _attn_sample(qkv_s, cache_k[l], cache_v[l])
    conv_s, nc_s = _conv_sample(state_conv[l], u_s.reshape(bs, t_new, conv_c), conv_w[l], cb, lg, lb)
    x1_s = _outproj(xs, attn_s.reshape(ms, attn_w), conv_s.reshape(ms, conv_c), w_out_b)
    y_s = _ffn(x1_s, g2, w_gate_b, w_up_b, w_down_b, gf)

    return (y_p.reshape(1, s, d), y_s.reshape(bs, t_new, d), nk_p, nv_p, nc_p,
            nk_s[None], nv_s[None], nc_s.reshape(1, bs, n_state, conv_c))
```

```python
import functools
import math

import jax
import jax.numpy as jnp
from jax import lax
from jax.experimental import pallas as pl
from jax.experimental.pallas import tpu as pltpu

F32 = jnp.float32
BF16 = jnp.bfloat16

HEAD_DIM = 128
ROPE_DIM = HEAD_DIM // 4
ROPE_THETA = 500000.0
CONV_K = 31
RMS_EPS = 1e-6
LN_EPS = 1e-5
PAST_LEN = 8192
DILATIONS = (1, 4, 16)
SPAN = 128
MAX_WINDOW = 2048
ATTN_TILE = MAX_WINDOW
ATTN_UNROLL = 8
CONV_HALO = 32
FFN_DOWN_CHUNK = 1024
NORM_ROWS = 64
WEIGHT_CAST_ROWS = 512
CACHE_ROLL_ROWS = 160

LANES = 128
VMEM_LIMIT_BYTES = 56 * 1024 * 1024


def _params(*sem):
    return pltpu.CompilerParams(dimension_semantics=sem, vmem_limit_bytes=VMEM_LIMIT_BYTES)


def _tile(n, pref, mult=8):
    if n <= pref:
        return n
    for t in range(pref - pref % mult, 0, -mult):
        if n % t == 0:
            return t
    raise ValueError(f"no tile for {n}")


def _rms_to_bf16(x, g):
    ms = jnp.mean(x * x, axis=-1, keepdims=True)
    return (x * lax.rsqrt(ms + RMS_EPS) * g).astype(BF16)


def _dot(a, b):
    return jnp.dot(a, b, preferred_element_type=F32)


def _dot_nt(a, b):
    return lax.dot_general(a, b, (((1,), (1,)), ((), ())), preferred_element_type=F32)


def _norm_kernel(x_ref, g_ref, h_ref):
    h_ref[...] = _rms_to_bf16(x_ref[...], g_ref[...])


def _cast_weight(w_ref, w_bf):
    k = w_ref.shape[0]
    rc = _tile(k, WEIGHT_CAST_ROWS)
    for r0 in range(0, k, rc):
        w_bf[r0:r0 + rc, :] = w_ref[r0:r0 + rc, :].astype(BF16)


def _row_parts(tm):
    half = tm // 2
    return ((0, tm),) if half % 16 else ((0, half), (half, tm))


def _qkv_kernel(h_ref, w_ref, tab_ref, o_ref, w_bf):
    @pl.when(pl.program_id(1) == 0)
    def _():
        _cast_weight(w_ref, w_bf)

    for r0, r1 in _row_parts(h_ref.shape[0]):
        z = _dot(h_ref[r0:r1, :], w_bf[...])
        c, s1, s2 = tab_ref[0, 0, r0:r1, :], tab_ref[0, 1, r0:r1, :], tab_ref[0, 2, r0:r1, :]
        for j in range(z.shape[1] // HEAD_DIM):
            zj = z[:, j * HEAD_DIM:(j + 1) * HEAD_DIM]
            lo = pltpu.roll(zj, ROPE_DIM // 2, 1)
            hi = pltpu.roll(zj, HEAD_DIM - ROPE_DIM // 2, 1)
            o_ref[r0:r1, j * HEAD_DIM:(j + 1) * HEAD_DIM] = zj * c + lo * s1 + hi * s2


def _glu_kernel(h_ref, wa_ref, wg_ref, u_ref, wa_bf, wg_bf):
    @pl.when(pl.program_id(1) == 0)
    def _():
        _cast_weight(wa_ref, wa_bf)
        _cast_weight(wg_ref, wg_bf)

    for r0, r1 in _row_parts(h_ref.shape[0]):
        h = h_ref[r0:r1, :]
        u_ref[r0:r1, :] = _dot(h, wa_bf[...]) * jax.nn.sigmoid(_dot(h, wg_bf[...]))


def _rope_tables(pos):
    half = ROPE_DIM // 2
    inv = ROPE_THETA ** (-jnp.arange(half, dtype=F32) / half)
    ang = pos.astype(F32)[:, None] * inv[None, :]
    cos, sin = jnp.cos(ang), jnp.sin(ang)
    m = pos.shape[0]
    one = jnp.ones((m, HEAD_DIM - ROPE_DIM), F32)
    zero = jnp.zeros((m, HEAD_DIM - ROPE_DIM), F32)
    zh = jnp.zeros((m, half), F32)
    c = jnp.concatenate([cos, cos, one], axis=1)
    s1 = jnp.concatenate([zh, sin, zero], axis=1)
    s2 = jnp.concatenate([-sin, zh, zero], axis=1)
    rot = jnp.stack([c, s1, s2])
    ident = jnp.stack([jnp.ones_like(c), jnp.zeros_like(c), jnp.zeros_like(c)])
    return jnp.stack([rot, ident])


def _project_in(x, g1, w_in, pos, attn_w, conv_c):
    m, d = x.shape
    tr = _tile(m, 256)
    h = pl.pallas_call(
        _norm_kernel,
        grid=(m // tr,),
        in_specs=[pl.BlockSpec((tr, d), lambda i: (i, 0)), pl.BlockSpec((1, d), lambda i: (0, 0))],
        out_specs=pl.BlockSpec((tr, d), lambda i: (i, 0)),
        out_shape=jax.ShapeDtypeStruct((m, d), BF16),
        compiler_params=_params("parallel"),
        name="norm1",
    )(x, g1)

    tm = _tile(m, 1024, 16)
    tn = _tile(attn_w, 1024, LANES)
    once = pl.Buffered(1) if m // tm > 1 else None
    n_rope = 2 * attn_w // tn
    qkv = pl.pallas_call(
        _qkv_kernel,
        grid=(3 * attn_w // tn, m // tm),
        in_specs=[pl.BlockSpec((tm, d), lambda n, i: (i, 0)),
                  pl.BlockSpec((d, tn), lambda n, i: (0, n), pipeline_mode=once),
                  pl.BlockSpec((1, 3, tm, HEAD_DIM), lambda n, i: (jnp.where(n < n_rope, 0, 1), 0, i, 0))],
        out_specs=pl.BlockSpec((tm, tn), lambda n, i: (i, n)),
        out_shape=jax.ShapeDtypeStruct((m, 3 * attn_w), F32),
        scratch_shapes=[pltpu.VMEM((d, tn), BF16)],
        compiler_params=_params("arbitrary", "arbitrary"),
        name="qkv_proj",
    )(h, w_in, _rope_tables(pos))

    tc = _tile(conv_c, 512, LANES)
    a0 = 3 * attn_w // tc
    g0 = (3 * attn_w + conv_c) // tc
    u = pl.pallas_call(
        _glu_kernel,
        grid=(conv_c // tc, m // tm),
        in_specs=[pl.BlockSpec((tm, d), lambda n, i: (i, 0)),
                  pl.BlockSpec((d, tc), lambda n, i: (0, a0 + n), pipeline_mode=once),
                  pl.BlockSpec((d, tc), lambda n, i: (0, g0 + n), pipeline_mode=once)],
        out_specs=pl.BlockSpec((tm, tc), lambda n, i: (i, n)),
        out_shape=jax.ShapeDtypeStruct((m, conv_c), F32),
        scratch_shapes=[pltpu.VMEM((d, tc), BF16)] * 2,
        compiler_params=_params("arbitrary", "arbitrary"),
        name="glu_proj",
    )(h, w_in, w_in)
    return qkv, u


def _rows(start, size, stride):
    return pl.ds(start, size) if stride == 1 else pl.ds(start, size, stride=stride)


def _attn_prompt_kernel(q_ref, kp_ref, kc_ref, vp_ref, vc_ref, *rest):
    n_w = (len(rest) - 4) // 2
    w_in, o_ref, w_out = rest[:n_w], rest[n_w], rest[n_w + 1:2 * n_w + 1]
    num_scr, m_scr, l_scr = rest[2 * n_w + 1:]
    for src, dst in zip(w_in, w_out):
        rc = _tile(src.shape[0], 16, 16)
        for r0 in range(0, src.shape[0], rc):
            dst[r0:r0 + rc, :] = src[r0:r0 + rc, :].astype(BF16)

    tile = pl.program_id(1)
    t = q_ref.shape[0]
    scale = HEAD_DIM ** -0.5
    qi = lax.broadcasted_iota(jnp.int32, (SPAN, 2 * SPAN), 0)
    kj = lax.broadcasted_iota(jnp.int32, (SPAN, 2 * SPAN), 1)
    band = (kj >= qi) & (kj <= qi + SPAN)
    first_key = jnp.where(tile > 0, 0, SPAN)
    band_first = band & (kj >= first_key)

    def attend(gi, d, q_start, q, k, v, mask):
        s = _dot_nt(q.astype(BF16), k.astype(BF16)) * scale
        s = jnp.where(mask, s, -jnp.inf)
        m = jnp.max(s, axis=-1, keepdims=True)
        p = jnp.exp(s - m)
        l = jnp.sum(p, axis=-1, keepdims=True)
        num = _dot(p.astype(BF16), v.astype(BF16))
        rows = _rows(q_start, SPAN, d)
        num_scr[gi, rows, :] = num
        m_scr[gi, rows, :] = jnp.broadcast_to(m, (SPAN, HEAD_DIM))
        l_scr[gi, rows, :] = jnp.broadcast_to(l, (SPAN, HEAD_DIM))

    def first_block(gi, d, r):
        q = q_ref[_rows(r, SPAN, d), :]
        k = jnp.concatenate([kp_ref[_rows(t - d * SPAN + r, SPAN, d), :],
                             kc_ref[_rows(r, SPAN, d), :]], axis=0)
        v = jnp.concatenate([vp_ref[_rows(t - d * SPAN + r, SPAN, d), :],
                             vc_ref[_rows(r, SPAN, d), :]], axis=0)
        attend(gi, d, r, q, k, v, band_first)

    def later_block(gi, d, r, b):
        q_start = r + d * SPAN * b
        if d == 1:
            q_start = pl.multiple_of(q_start, SPAN)
        k_rows = _rows(q_start - d * SPAN, 2 * SPAN, d)
        attend(gi, d, q_start, q_ref[_rows(q_start, SPAN, d), :], kc_ref[k_rows, :], vc_ref[k_rows, :], band)

    u = ATTN_UNROLL
    for gi, d in enumerate(DILATIONS):
        n_blocks = t // (d * SPAN)
        if n_blocks >= u:
            assert n_blocks % u == 0

            def class_body(r, carry, gi=gi, d=d, n_blocks=n_blocks):
                first_block(gi, d, r)
                for b in range(1, u):
                    later_block(gi, d, r, b)

                def rest(i, c):
                    for j in range(u):
                        later_block(gi, d, r, i * u + j)
                    return c

                if n_blocks > u:
                    lax.fori_loop(1, n_blocks // u, rest, 0)
                return carry

            if d == 1:
                class_body(0, 0)
            else:
                lax.fori_loop(0, d, class_body, 0)
        else:
            per_iter = u // n_blocks
            assert u % n_blocks == 0 and d % per_iter == 0

            def class_group(i, carry, gi=gi, d=d, n_blocks=n_blocks, per_iter=per_iter):
                for j in range(per_iter):
                    r = i * per_iter + j
                    first_block(gi, d, r)
                    for b in range(1, n_blocks):
                        later_block(gi, d, r, b)
                return carry

            lax.fori_loop(0, d // per_iter, class_group, 0)

    def combine(c, carry):
        rows = pl.ds(pl.multiple_of(c * SPAN, SPAN), SPAN)
        ms = [m_scr[g, rows, :] for g in range(len(DILATIONS))]
        m_all = functools.reduce(jnp.maximum, ms)
        num = jnp.zeros((SPAN, HEAD_DIM), F32)
        den = jnp.zeros((SPAN, HEAD_DIM), F32)
        for g in range(len(DILATIONS)):
            sc = jnp.exp(ms[g] - m_all)
            num = num + num_scr[g, rows, :] * sc
            den = den + l_scr[g, rows, :] * sc
        o_ref[rows, :] = (num / den).astype(o_ref.dtype)
        return carry

    lax.fori_loop(0, t // SPAN, combine, 0)


def _attn_prompt(qkv, n_heads, weights=()):
    s = qkv.shape[0]
    t = ATTN_TILE
    assert s % t == 0, "prompt length must be a multiple of the dilation tile"
    n_tiles = s // t
    cur = lambda off: pl.BlockSpec((t, HEAD_DIM), lambda h, i: (i, off + h))
    prev = lambda off: pl.BlockSpec((t, HEAD_DIM), lambda h, i: (jnp.maximum(i - 1, 0), off + h))
    w_specs, w_shapes = [], []
    for w in weights:
        rows, cols = w.shape
        rb = next(r for r in range(16, rows + 1, 16) if rows % r == 0 and rows // r <= n_heads * n_tiles)
        spec = pl.BlockSpec((rb, cols), lambda h, i, nb=rows // rb: (jnp.minimum(h * n_tiles + i, nb - 1), 0))
        w_specs.append(spec)
        w_shapes.append(jax.ShapeDtypeStruct(w.shape, BF16))
    out = pl.pallas_call(
        _attn_prompt_kernel,
        grid=(n_heads, n_tiles),
        in_specs=[cur(0), prev(n_heads), cur(n_heads), prev(2 * n_heads), cur(2 * n_heads)] + w_specs,
        out_specs=[pl.BlockSpec((t, HEAD_DIM), lambda h, i: (i, h))] + w_specs,
        out_shape=[jax.ShapeDtypeStruct((s, n_heads * HEAD_DIM), BF16)] + w_shapes,
        scratch_shapes=[pltpu.VMEM((len(DILATIONS), t, HEAD_DIM), F32)] * 3,
        compiler_params=_params("arbitrary", "arbitrary"),
        name="attn_prompt",
    )(qkv, qkv, qkv, qkv, qkv, *weights)
    return out[0], tuple(out[1:])


def _attn_sample_kernel(qkv_ref, kt_ref, vt_ref, kf_ref, vf_ref, o_ref, kn_scr, vn_scr, *, n_heads):
    n_new = qkv_ref.shape[1]
    tail = kt_ref.shape[1]
    d_far = DILATIONS[-1]
    kn_scr[0:tail] = kt_ref[0]
    vn_scr[0:tail] = vt_ref[0]
    kn_scr[tail:tail + n_new] = qkv_ref[0, :, n_heads:2 * n_heads, :]
    vn_scr[tail:tail + n_new] = qkv_ref[0, :, 2 * n_heads:3 * n_heads, :]
    n_near_far = tail // d_far + 1

    for i in range(n_new):
        q = qkv_ref[0, i, 0:n_heads, :] * (HEAD_DIM ** -0.5)
        lists = []
        for d in DILATIONS[:-1]:
            rows = _rows(tail + i - d * SPAN, SPAN + 1, d)
            lists.append((kn_scr[rows], vn_scr[rows]))
        rows = _rows(tail + i - d_far * (n_near_far - 1), n_near_far, d_far)
        lists.append((kn_scr[rows], vn_scr[rows]))
        lists.append((kf_ref[0, :, i], vf_ref[0, :, i]))
        scores = [jnp.sum(k * q[None], axis=-1, keepdims=True) for k, _ in lists]
        m = functools.reduce(jnp.maximum, [jnp.max(s, axis=0) for s in scores])
        den = jnp.zeros((n_heads, 1), F32)
        num = jnp.zeros((n_heads, HEAD_DIM), F32)
        for s, (_, v) in zip(scores, lists):
            p = jnp.exp(s - m[None])
            den = den + jnp.sum(p, axis=0)
            num = num + jnp.sum(p * v, axis=0)
        o_ref[0, i] = num / den


def _attn_sample(qkv, cache_k, cache_v):
    b, n_new, _, _ = qkv.shape
    buf, n_heads = cache_k.shape[1], cache_k.shape[2]
    d_mid, d_far = DILATIONS[1], DILATIONS[2]
    tail = d_mid * SPAN
    assert buf == d_far * SPAN and n_new <= d_mid and tail % d_far == 0
    n_far = (buf - tail) // d_far
    far_shape = (b, buf // d_far, d_far, n_heads, HEAD_DIM)
    near = pl.BlockSpec((1, tail, n_heads, HEAD_DIM), lambda i: (i, buf // tail - 1, 0, 0))
    far = pl.BlockSpec((1, n_far, n_new, n_heads, HEAD_DIM), lambda i: (i, 0, 0, 0, 0))
    new = pl.BlockSpec((1, n_new, 3 * n_heads, HEAD_DIM), lambda i: (i, 0, 0, 0))
    ctx = pltpu.VMEM((tail + 8, n_heads, HEAD_DIM), F32)
    return pl.pallas_call(
        functools.partial(_attn_sample_kernel, n_heads=n_heads),
        grid=(b,),
        in_specs=[new, near, near, far, far],
        out_specs=pl.BlockSpec((1, n_new, n_heads, HEAD_DIM), lambda i: (i, 0, 0, 0)),
        out_shape=jax.ShapeDtypeStruct((b, n_new, n_heads, HEAD_DIM), F32),
        scratch_shapes=[ctx, ctx],
        compiler_params=_params("parallel"),
        name="attn_sample",
    )(qkv, cache_k, cache_v, cache_k.reshape(far_shape), cache_v.reshape(far_shape))


def _ln_swish(y, g, b):
    mu = jnp.mean(y, axis=-1, keepdims=True)
    yc = y - mu
    var = jnp.mean(yc * yc, axis=-1, keepdims=True)
    z = yc * lax.rsqrt(var + LN_EPS) * g + b
    return z * jax.nn.sigmoid(z)


def _conv_prompt_kernel(up_ref, u_ref, w_ref, b_ref, lg_ref, lb_ref, o_ref, ctx_scr, y_scr):
    tm, c = u_ref.shape
    n_slabs = c // LANES
    slab = lambda cs: slice(cs * LANES, (cs + 1) * LANES)

    @pl.when(pl.program_id(0) > 0)
    def _():
        for cs in range(n_slabs):
            ctx_scr[cs, 0:CONV_HALO, :] = up_ref[:, slab(cs)]

    @pl.when(pl.program_id(0) == 0)
    def _():
        for cs in range(n_slabs):
            ctx_scr[cs, 0:CONV_HALO, :] = jnp.zeros((CONV_HALO, LANES), F32)

    for cs in range(n_slabs):
        ctx_scr[cs, CONV_HALO:CONV_HALO + tm, :] = u_ref[:, slab(cs)]

    off = CONV_HALO - (CONV_K - 1)
    rc = 32

    for cs in range(n_slabs):
        taps = [jnp.broadcast_to(w_ref[j:j + 1, slab(cs)], (rc, LANES)) for j in range(CONV_K)]
        bias = jnp.broadcast_to(b_ref[:, slab(cs)], (rc, LANES))

        def chunk(i, carry, cs=cs, taps=taps, bias=bias):
            r0 = i * (2 * rc)
            for phase in range(2):
                acc = bias
                for j in range(CONV_K):
                    acc = acc + ctx_scr[cs, pl.ds(r0 + phase + off + j, rc, stride=2), :] * taps[j]
                y_scr[cs, pl.ds(r0 + phase, rc, stride=2), :] = acc
            return carry

        lax.fori_loop(0, tm // (2 * rc), chunk, 0)

    ln_rows = 16
    ln_unroll = 4

    def ln_rows_at(row0):
        rows = pl.ds(pl.multiple_of(row0, ln_rows), ln_rows)
        ys = [y_scr[cs, rows, :] for cs in range(n_slabs)]
        mu = jnp.sum(functools.reduce(jnp.add, ys), axis=-1, keepdims=True) * (1.0 / c)
        ycs = [y - mu for y in ys]
        var = jnp.sum(functools.reduce(jnp.add, [yc * yc for yc in ycs]), axis=-1, keepdims=True) * (1.0 / c)
        inv = lax.rsqrt(var + LN_EPS)
        for cs in range(n_slabs):
            z = ycs[cs] * inv * lg_ref[:, slab(cs)] + lb_ref[:, slab(cs)]
            o_ref[rows, slab(cs)] = (z * jax.nn.sigmoid(z)).astype(o_ref.dtype)

    def ln_chunk(i, carry):
        for j in range(ln_unroll):
            ln_rows_at((i * ln_unroll + j) * ln_rows)
        return carry

    lax.fori_loop(0, tm // (ln_rows * ln_unroll), ln_chunk, 0)


def _conv_prompt(u, conv_w, conv_b, ln_g, ln_b):
    s, c = u.shape
    tm = _tile(s, 256, CONV_HALO)
    vec = pl.BlockSpec((1, c), lambda i: (0, 0))
    return pl.pallas_call(
        _conv_prompt_kernel,
        grid=(s // tm,),
        in_specs=[pl.BlockSpec((CONV_HALO, c), lambda i: (jnp.maximum(i * (tm // CONV_HALO) - 1, 0), 0)),
                  pl.BlockSpec((tm, c), lambda i: (i, 0)),
                  pl.BlockSpec((CONV_K, c), lambda i: (0, 0)), vec, vec, vec],
        out_specs=pl.BlockSpec((tm, c), lambda i: (i, 0)),
        out_shape=jax.ShapeDtypeStruct((s, c), BF16),
        scratch_shapes=[pltpu.VMEM((c // LANES, CONV_HALO + tm, LANES), F32),
                        pltpu.VMEM((c // LANES, tm, LANES), F32)],
        compiler_params=_params("parallel"),
        name="conv_prompt",
    )(u, u, conv_w, conv_b, ln_g, ln_b)


def _conv_sample_kernel(st_ref, u_ref, w_ref, b_ref, lg_ref, lb_ref, o_ref, ns_ref, ctx_scr):
    n_state, n_new = st_ref.shape[1], u_ref.shape[1]
    ctx_scr[0:n_state, :] = st_ref[0]
    ctx_scr[n_state:n_state + n_new, :] = u_ref[0]
    acc = jnp.zeros(u_ref.shape[1:], F32)
    for j in range(CONV_K):
        acc = acc + ctx_scr[j:j + n_new, :] * w_ref[j:j + 1, :]
    o_ref[0] = _ln_swish(acc + b_ref[...], lg_ref[...], lb_ref[...])
    ns_ref[0] = ctx_scr[n_new:n_new + n_state, :]


def _conv_sample(state, u, conv_w, conv_b, ln_g, ln_b):
    b, n_state, c = state.shape
    n_new = u.shape[1]
    assert n_state == CONV_K - 1
    vec = pl.BlockSpec((1, c), lambda i: (0, 0))
    return pl.pallas_call(
        _conv_sample_kernel,
        grid=(b,),
        in_specs=[pl.BlockSpec((1, n_state, c), lambda i: (i, 0, 0)),
                  pl.BlockSpec((1, n_new, c), lambda i: (i, 0, 0)),
                  pl.BlockSpec((CONV_K, c), lambda i: (0, 0)), vec, vec, vec],
        out_specs=[pl.BlockSpec((1, n_new, c), lambda i: (i, 0, 0)),
                   pl.BlockSpec((1, n_state, c), lambda i: (i, 0, 0))],
        out_shape=[jax.ShapeDtypeStruct((b, n_new, c), F32),
                   jax.ShapeDtypeStruct((b, n_state, c), F32)],
        scratch_shapes=[pltpu.VMEM((n_state + n_new, c), F32)],
        compiler_params=_params("parallel"),
        name="conv_sample",
    )(state, u, conv_w, conv_b, ln_g, ln_b)


def _outproj_kernel(x_ref, a_ref, c_ref, wa_ref, wc_ref, o_ref, wa_bf, wc_bf):
    @pl.when(pl.program_id(1) == 0)
    def _():
        _cast_weight(wa_ref, wa_bf)
        _cast_weight(wc_ref, wc_bf)

    for r0, r1 in _row_parts(x_ref.shape[0]):
        acc = _dot(a_ref[r0:r1, :].astype(BF16), wa_bf[...]) + _dot(c_ref[r0:r1, :].astype(BF16), wc_bf[...])
        o_ref[r0:r1, :] = x_ref[r0:r1, :] + acc


def _outproj(x, attn, conv, w_out):
    m, d = x.shape
    a, c = attn.shape[1], conv.shape[1]
    assert a == c
    tm = _tile(m, 1024, 16)
    tn = _tile(d, 512, LANES)
    return pl.pallas_call(
        _outproj_kernel,
        grid=(d // tn, m // tm),
        in_specs=[pl.BlockSpec((tm, tn), lambda n, i: (i, n)),
                  pl.BlockSpec((tm, a), lambda n, i: (i, 0)),
                  pl.BlockSpec((tm, c), lambda n, i: (i, 0)),
                  pl.BlockSpec((a, tn), lambda n, i: (0, n)),
                  pl.BlockSpec((c, tn), lambda n, i: (1, n))],
        out_specs=pl.BlockSpec((tm, tn), lambda n, i: (i, n)),
        out_shape=jax.ShapeDtypeStruct((m, d), F32),
        scratch_shapes=[pltpu.VMEM((a, tn), BF16), pltpu.VMEM((c, tn), BF16)],
        compiler_params=_params("arbitrary", "arbitrary"),
        name="out_proj",
    )(x, attn, conv, w_out, w_out)


def _ffn_kernel(x_ref, g2_ref, wg_ref, wu_ref, wd_ref, gf_ref, *rest, roll):
    if roll is None:
        y_ref, h_ref = rest
    else:
        ck_ref, cv_ref, new_ref, y_ref, nk_ref, nv_ref, h_ref, slots, in_sem, out_sem, new_sem = rest
    f = pl.program_id(1)
    if roll is not None:
        step = pl.program_id(0) * pl.num_programs(1) + f
        _cache_roll_step(step, roll, (ck_ref, cv_ref), new_ref, (nk_ref, nv_ref), slots, in_sem, out_sem, new_sem)
    tm, d = y_ref.shape
    rc = _tile(tm, NORM_ROWS)

    def by_row_chunks(body):
        def step(c, carry):
            body(pl.ds(pl.multiple_of(c * rc, rc), rc))
            return carry
        lax.fori_loop(0, tm // rc, step, 0)

    @pl.when(f == 0)
    def _():
        def body(rows):
            x = x_ref[rows, :]
            h_ref[rows, :] = _rms_to_bf16(x, g2_ref[...])
            y_ref[rows, :] = x
        by_row_chunks(body)

    h = h_ref[...]
    gate = _dot(h, wg_ref[...])
    act = (gate * jax.nn.sigmoid(gate) * _dot(h, wu_ref[...])).astype(BF16)
    dc = _tile(d, FFN_DOWN_CHUNK, LANES)
    for c0 in range(0, d, dc):
        y_ref[:, c0:c0 + dc] += _dot(act, wd_ref[:, c0:c0 + dc])

    @pl.when(f == pl.num_programs(1) - 1)
    def _():
        def body(rows):
            y = y_ref[rows, :]
            ms = jnp.mean(y * y, axis=-1, keepdims=True)
            y_ref[rows, :] = y * lax.rsqrt(ms + RMS_EPS) * gf_ref[...]
        by_row_chunks(body)


def _cache_roll_step(step, roll, caches, new_ref, rolled, slots, in_sem, out_sem, new_sem):
    n_chunks, per_seq, n_heads = roll
    rows = slots.shape[2]
    n_new = new_ref.shape[1]

    def chunk_copies(c):
        seq, j, slot = c // per_seq, c % per_seq, c % 2
        pairs = []
        for i in range(2):
            buf = slots.at[slot, i]
            pairs.append((
                pltpu.make_async_copy(caches[i].at[seq, pl.ds(j * rows + n_new, rows)], buf, in_sem.at[slot, i]),
                pltpu.make_async_copy(buf, rolled[i].at[seq, pl.ds(j * rows, rows)], out_sem.at[slot, i])))
        return pairs

    def new_copies(c):
        seq = c // per_seq
        return [pltpu.make_async_copy(new_ref.at[seq, :, pl.ds((i + 1) * n_heads, n_heads)],
                                      rolled[i].at[seq, pl.ds(per_seq * rows, n_new)], new_sem.at[i])
                for i in range(2)]

    @pl.when((step >= 2) & (step < n_chunks + 2))
    def _():
        for _, out in chunk_copies(step - 2):
            out.wait()

    @pl.when(step < n_chunks)
    def _():
        for inp, _ in chunk_copies(step):
            inp.start()

    @pl.when((step >= 1) & (step < n_chunks + 1))
    def _():
        for inp, out in chunk_copies(step - 1):
            inp.wait()
            out.start()

    @pl.when((step >= 1) & (step < n_chunks + 1) & ((step - 1) % per_seq == 0))
    def _():
        for cp in new_copies(step - 1):
            cp.wait()

    @pl.when((step < n_chunks) & (step % per_seq == 0))
    def _():
        for cp in new_copies(step):
            cp.start()


def _ffn(x, g2, w_gate, w_up, w_down, gf, caches=None):
    m, d = x.shape
    dff = w_gate.shape[1]
    tm = _tile(m, 512)
    tf = _tile(dff, 256, LANES)
    nf = dff // tf
    grid = (m // tm, nf)
    vec = pl.BlockSpec((1, d), lambda i, f: (0, 0))
    in_specs = [pl.BlockSpec((tm, d), lambda i, f: (i, 0), pipeline_mode=pl.Buffered(1)), vec,
                pl.BlockSpec((d, tf), lambda i, f: (0, f)),
                pl.BlockSpec((d, tf), lambda i, f: (0, f)),
                pl.BlockSpec((tf, d), lambda i, f: (f, 0)), vec]
    out_specs = [pl.BlockSpec((tm, d), lambda i, f: (i, 0))]
    out_shape = [jax.ShapeDtypeStruct((m, d), F32)]
    args = [x, g2, w_gate, w_up, w_down, gf]
    roll = None
    scratch = [pltpu.VMEM((tm, d), BF16)]
    if caches is not None:
        cache_k, cache_v, qkv_new = caches
        b, buf, n_heads, _ = cache_k.shape
        n_new = qkv_new.shape[1]
        assert buf == MAX_WINDOW, "the rolled cache keeps exactly the window"
        old = buf - n_new
        per_seq = next(p for p in range(2, old + 1) if old % p == 0 and old // p <= CACHE_ROLL_ROWS)
        n_chunks = b * per_seq
        assert n_chunks + 2 <= grid[0] * grid[1]
        any_spec = pl.BlockSpec(memory_space=pl.ANY)
        in_specs += [any_spec] * 3
        out_specs += [any_spec] * 2
        out_shape += [jax.ShapeDtypeStruct((b, buf, n_heads, HEAD_DIM), F32)] * 2
        args += [cache_k, cache_v, qkv_new]
        scratch += [pltpu.VMEM((2, 2, old // per_seq, n_heads, HEAD_DIM), F32),
                    pltpu.SemaphoreType.DMA((2, 2)), pltpu.SemaphoreType.DMA((2, 2)),
                    pltpu.SemaphoreType.DMA((2,))]
        roll = (n_chunks, per_seq, n_heads)
    out = pl.pallas_call(
        functools.partial(_ffn_kernel, roll=roll),
        grid=grid,
        in_specs=in_specs,
        out_specs=out_specs,
        out_shape=out_shape,
        scratch_shapes=scratch,
        compiler_params=_params("arbitrary", "arbitrary"),
        name="ffn",
    )(*args)
    return out[0] if caches is None else out


def kernel(x_prompt, x_sample, cache_k, cache_v, state_conv, norm1_g, w_in, conv_w, conv_b, conv_ln_g,
           conv_ln_b, w_out, norm2_g, w_gate, w_up, w_down, final_g):
    depth = w_in.shape[0]
    bp, s, d = x_prompt.shape
    bs, t_new, _ = x_sample.shape
    assert depth == 1 and bp == 1, "one layer, one prompt sequence"
    attn_w = d // 2
    conv_c = d - attn_w
    n_heads = attn_w // HEAD_DIM
    buf = cache_k.shape[2]

    row = lambda v: v.reshape(1, -1)
    l = 0
    w_in_b, w_out_b = w_in[l], w_out[l]
    g1, g2, gf = row(norm1_g[l]), row(norm2_g[l]), row(final_g)
    cb, lg, lb = row(conv_b[l]), row(conv_ln_g[l]), row(conv_ln_b[l])

    ms = bs * t_new
    xs = x_sample.reshape(ms, d)
    pos_s = PAST_LEN + jnp.tile(jnp.arange(t_new, dtype=jnp.int32), bs)
    qkv_s, u_s = _project_in(xs, g1, w_in_b, pos_s, attn_w, conv_c)
    qkv_s = qkv_s.reshape(bs, t_new, 3 * n_heads, HEAD_DIM)

    xp = x_prompt.reshape(s, d)
    qkv_p, u_p = _project_in(xp, g1, w_in_b, jnp.arange(s, dtype=jnp.int32), attn_w, conv_c)
    attn_p, (w_gate_b, w_up_b, w_down_b) = _attn_prompt(qkv_p, n_heads, (w_gate[l], w_up[l], w_down[l]))
    conv_p = _conv_prompt(u_p, conv_w[l], cb, lg, lb)
    x1_p = _outproj(xp, attn_p, conv_p, w_out_b)
    y_p, nk_s, nv_s = _ffn(x1_p, g2, w_gate_b, w_up_b, w_down_b, gf, caches=(cache_k[l], cache_v[l], qkv_s))
    keep_p = min(MAX_WINDOW, s)
    nk_p = qkv_p[s - keep_p:, attn_w:2 * attn_w].reshape(1, 1, keep_p, n_heads, HEAD_DIM)
    nv_p = qkv_p[s - keep_p:, 2 * attn_w:].reshape(1, 1, keep_p, n_heads, HEAD_DIM)
    n_state = CONV_K - 1
    assert s >= n_state
    nc_p = u_p[s - n_state:].reshape(1, 1, n_state, conv_c)

    attn_s = _attn_sample(qkv_s, cache_k[l], cache_v[l])
    conv_s, nc_s = _conv_sample(state_conv[l], u_s.reshape(bs, t_new, conv_c), conv_w[l], cb, lg, lb)
    x1_s = _outproj(xs, attn_s.reshape(ms, attn_w), conv_s.reshape(ms, conv_c), w_out_b)
    y_s = _ffn(x1_s, g2, w_gate_b, w_up_b, w_down_b, gf)

    return (y_p.reshape(1, s, d), y_s.reshape(bs, t_new, d), nk_p, nv_p, nc_p,
            nk_s[None], nv_s[None], nc_s.reshape(1, bs, n_state, conv_c))
```

```python
import functools
import math

import jax
import jax.numpy as jnp
from jax import lax
from jax.experimental import pallas as pl
from jax.experimental.pallas import tpu as pltpu

F32 = jnp.float32
BF16 = jnp.bfloat16

HEAD_DIM = 128
ROPE_DIM = HEAD_DIM // 4
ROPE_THETA = 500000.0
CONV_K = 31
RMS_EPS = 1e-6
LN_EPS = 1e-5
PAST_LEN = 8192
DILATIONS = (1, 4, 16)
SPAN = 128
MAX_WINDOW = 2048
ATTN_TILE = MAX_WINDOW
ATTN_UNROLL = 8
CONV_HALO = 32
FFN_DOWN_CHUNK = 1024
NORM_ROWS = 64
WEIGHT_CAST_ROWS = 512

LANES = 128
VMEM_LIMIT_BYTES = 56 * 1024 * 1024


def _params(*sem):
    return pltpu.CompilerParams(dimension_semantics=sem, vmem_limit_bytes=VMEM_LIMIT_BYTES)


def _tile(n, pref, mult=8):
    if n <= pref:
        return n
    for t in range(pref - pref % mult, 0, -mult):
        if n % t == 0:
            return t
    raise ValueError(f"no tile for {n}")


def _rms_to_bf16(x, g):
    ms = jnp.mean(x * x, axis=-1, keepdims=True)
    return (x * lax.rsqrt(ms + RMS_EPS) * g).astype(BF16)


def _dot(a, b):
    return jnp.dot(a, b, preferred_element_type=F32)


def _dot_nt(a, b):
    return lax.dot_general(a, b, (((1,), (1,)), ((), ())), preferred_element_type=F32)


def _norm_kernel(x_ref, g_ref, h_ref):
    h_ref[...] = _rms_to_bf16(x_ref[...], g_ref[...])


def _cast_weight(w_ref, w_bf):
    k = w_ref.shape[0]
    rc = _tile(k, WEIGHT_CAST_ROWS)
    for r0 in range(0, k, rc):
        w_bf[r0:r0 + rc, :] = w_ref[r0:r0 + rc, :].astype(BF16)


def _row_parts(tm):
    half = tm // 2
    return ((0, tm),) if half % 16 else ((0, half), (half, tm))


def _qkv_kernel(h_ref, w_ref, tab_ref, o_ref, w_bf):
    @pl.when(pl.program_id(1) == 0)
    def _():
        _cast_weight(w_ref, w_bf)

    for r0, r1 in _row_parts(h_ref.shape[0]):
        z = _dot(h_ref[r0:r1, :], w_bf[...])
        c, s1, s2 = tab_ref[0, 0, r0:r1, :], tab_ref[0, 1, r0:r1, :], tab_ref[0, 2, r0:r1, :]
        for j in range(z.shape[1] // HEAD_DIM):
            zj = z[:, j * HEAD_DIM:(j + 1) * HEAD_DIM]
            lo = pltpu.roll(zj, ROPE_DIM // 2, 1)
            hi = pltpu.roll(zj, HEAD_DIM - ROPE_DIM // 2, 1)
            o_ref[r0:r1, j * HEAD_DIM:(j + 1) * HEAD_DIM] = zj * c + lo * s1 + hi * s2


def _glu_kernel(h_ref, wa_ref, wg_ref, u_ref, wa_bf, wg_bf):
    @pl.when(pl.program_id(1) == 0)
    def _():
        _cast_weight(wa_ref, wa_bf)
        _cast_weight(wg_ref, wg_bf)

    for r0, r1 in _row_parts(h_ref.shape[0]):
        h = h_ref[r0:r1, :]
        u_ref[r0:r1, :] = _dot(h, wa_bf[...]) * jax.nn.sigmoid(_dot(h, wg_bf[...]))


def _rope_tables(pos):
    half = ROPE_DIM // 2
    inv = ROPE_THETA ** (-jnp.arange(half, dtype=F32) / half)
    ang = pos.astype(F32)[:, None] * inv[None, :]
    cos, sin = jnp.cos(ang), jnp.sin(ang)
    m = pos.shape[0]
    one = jnp.ones((m, HEAD_DIM - ROPE_DIM), F32)
    zero = jnp.zeros((m, HEAD_DIM - ROPE_DIM), F32)
    zh = jnp.zeros((m, half), F32)
    c = jnp.concatenate([cos, cos, one], axis=1)
    s1 = jnp.concatenate([zh, sin, zero], axis=1)
    s2 = jnp.concatenate([-sin, zh, zero], axis=1)
    rot = jnp.stack([c, s1, s2])
    ident = jnp.stack([jnp.ones_like(c), jnp.zeros_like(c), jnp.zeros_like(c)])
    return jnp.stack([rot, ident])


def _project_in(x, g1, w_in, pos, attn_w, conv_c):
    m, d = x.shape
    tr = _tile(m, 256)
    h = pl.pallas_call(
        _norm_kernel,
        grid=(m // tr,),
        in_specs=[pl.BlockSpec((tr, d), lambda i: (i, 0)), pl.BlockSpec((1, d), lambda i: (0, 0))],
        out_specs=pl.BlockSpec((tr, d), lambda i: (i, 0)),
        out_shape=jax.ShapeDtypeStruct((m, d), BF16),
        compiler_params=_params("parallel"),
        name="norm1",
    )(x, g1)

    tm = _tile(m, 1024, 16)
    tn = _tile(attn_w, 1024, LANES)
    once = pl.Buffered(1) if m // tm > 1 else None
    n_rope = 2 * attn_w // tn
    qkv = pl.pallas_call(
        _qkv_kernel,
        grid=(3 * attn_w // tn, m // tm),
        in_specs=[pl.BlockSpec((tm, d), lambda n, i: (i, 0)),
                  pl.BlockSpec((d, tn), lambda n, i: (0, n), pipeline_mode=once),
                  pl.BlockSpec((1, 3, tm, HEAD_DIM), lambda n, i: (jnp.where(n < n_rope, 0, 1), 0, i, 0))],
        out_specs=pl.BlockSpec((tm, tn), lambda n, i: (i, n)),
        out_shape=jax.ShapeDtypeStruct((m, 3 * attn_w), F32),
        scratch_shapes=[pltpu.VMEM((d, tn), BF16)],
        compiler_params=_params("arbitrary", "arbitrary"),
        name="qkv_proj",
    )(h, w_in, _rope_tables(pos))

    tc = _tile(conv_c, 512, LANES)
    a0 = 3 * attn_w // tc
    g0 = (3 * attn_w + conv_c) // tc
    u = pl.pallas_call(
        _glu_kernel,
        grid=(conv_c // tc, m // tm),
        in_specs=[pl.BlockSpec((tm, d), lambda n, i: (i, 0)),
                  pl.BlockSpec((d, tc), lambda n, i: (0, a0 + n), pipeline_mode=once),
                  pl.BlockSpec((d, tc), lambda n, i: (0, g0 + n), pipeline_mode=once)],
        out_specs=pl.BlockSpec((tm, tc), lambda n, i: (i, n)),
        out_shape=jax.ShapeDtypeStruct((m, conv_c), F32),
        scratch_shapes=[pltpu.VMEM((d, tc), BF16)] * 2,
        compiler_params=_params("arbitrary", "arbitrary"),
        name="glu_proj",
    )(h, w_in, w_in)
    return qkv, u


def _rows(start, size, stride):
    return pl.ds(start, size) if stride == 1 else pl.ds(start, size, stride=stride)


def _attn_prompt_kernel(q_ref, kp_ref, kc_ref, vp_ref, vc_ref, *rest):
    n_w = (len(rest) - 4) // 2
    w_in, o_ref, w_out = rest[:n_w], rest[n_w], rest[n_w + 1:2 * n_w + 1]
    num_scr, m_scr, l_scr = rest[2 * n_w + 1:]
    for src, dst in zip(w_in, w_out):
        rc = _tile(src.shape[0], 16, 16)
        for r0 in range(0, src.shape[0], rc):
            dst[r0:r0 + rc, :] = src[r0:r0 + rc, :].astype(BF16)

    tile = pl.program_id(1)
    t = q_ref.shape[0]
    scale = HEAD_DIM ** -0.5
    qi = lax.broadcasted_iota(jnp.int32, (SPAN, 2 * SPAN), 0)
    kj = lax.broadcasted_iota(jnp.int32, (SPAN, 2 * SPAN), 1)
    band = (kj >= qi) & (kj <= qi + SPAN)
    first_key = jnp.where(tile > 0, 0, SPAN)
    band_first = band & (kj >= first_key)

    def attend(gi, d, q_start, q, k, v, mask):
        s = _dot_nt(q.astype(BF16), k.astype(BF16)) * scale
        s = jnp.where(mask, s, -jnp.inf)
        m = jnp.max(s, axis=-1, keepdims=True)
        p = jnp.exp(s - m)
        l = jnp.sum(p, axis=-1, keepdims=True)
        num = _dot(p.astype(BF16), v.astype(BF16))
        rows = _rows(q_start, SPAN, d)
        num_scr[gi, rows, :] = num
        m_scr[gi, rows, :] = jnp.broadcast_to(m, (SPAN, HEAD_DIM))
        l_scr[gi, rows, :] = jnp.broadcast_to(l, (SPAN, HEAD_DIM))

    def first_block(gi, d, r):
        q = q_ref[_rows(r, SPAN, d), :]
        k = jnp.concatenate([kp_ref[_rows(t - d * SPAN + r, SPAN, d), :],
                             kc_ref[_rows(r, SPAN, d), :]], axis=0)
        v = jnp.concatenate([vp_ref[_rows(t - d * SPAN + r, SPAN, d), :],
                             vc_ref[_rows(r, SPAN, d), :]], axis=0)
        attend(gi, d, r, q, k, v, band_first)

    def later_block(gi, d, r, b):
        q_start = r + d * SPAN * b
        if d == 1:
            q_start = pl.multiple_of(q_start, SPAN)
        k_rows = _rows(q_start - d * SPAN, 2 * SPAN, d)
        attend(gi, d, q_start, q_ref[_rows(q_start, SPAN, d), :], kc_ref[k_rows, :], vc_ref[k_rows, :], band)

    u = ATTN_UNROLL
    for gi, d in enumerate(DILATIONS):
        n_blocks = t // (d * SPAN)
        if n_blocks >= u:
            assert n_blocks % u == 0

            def class_body(r, carry, gi=gi, d=d, n_blocks=n_blocks):
                first_block(gi, d, r)
                for b in range(1, u):
                    later_block(gi, d, r, b)

                def rest(i, c):
                    for j in range(u):
                        later_block(gi, d, r, i * u + j)
                    return c

                if n_blocks > u:
                    lax.fori_loop(1, n_blocks // u, rest, 0)
                return carry

            if d == 1:
                class_body(0, 0)
            else:
                lax.fori_loop(0, d, class_body, 0)
        else:
            per_iter = u // n_blocks
            assert u % n_blocks == 0 and d % per_iter == 0

            def class_group(i, carry, gi=gi, d=d, n_blocks=n_blocks, per_iter=per_iter):
                for j in range(per_iter):
                    r = i * per_iter + j
                    first_block(gi, d, r)
                    for b in range(1, n_blocks):
                        later_block(gi, d, r, b)
                return carry

            lax.fori_loop(0, d // per_iter, class_group, 0)

    def combine(c, carry):
        rows = pl.ds(pl.multiple_of(c * SPAN, SPAN), SPAN)
        ms = [m_scr[g, rows, :] for g in range(len(DILATIONS))]
        m_all = functools.reduce(jnp.maximum, ms)
        num = jnp.zeros((SPAN, HEAD_DIM), F32)
        den = jnp.zeros((SPAN, HEAD_DIM), F32)
        for g in range(len(DILATIONS)):
            sc = jnp.exp(ms[g] - m_all)
            num = num + num_scr[g, rows, :] * sc
            den = den + l_scr[g, rows, :] * sc
        o_ref[rows, :] = (num / den).astype(o_ref.dtype)
        return carry

    lax.fori_loop(0, t // SPAN, combine, 0)


def _attn_prompt(qkv, n_heads, weights=()):
    s = qkv.shape[0]
    t = ATTN_TILE
    assert s % t == 0, "prompt length must be a multiple of the dilation tile"
    n_tiles = s // t
    cur = lambda off: pl.BlockSpec((t, HEAD_DIM), lambda h, i: (i, off + h))
    prev = lambda off: pl.BlockSpec((t, HEAD_DIM), lambda h, i: (jnp.maximum(i - 1, 0), off + h))
    w_specs, w_shapes = [], []
    for w in weights:
        rows, cols = w.shape
        rb = next(r for r in range(16, rows + 1, 16) if rows % r == 0 and rows // r <= n_heads * n_tiles)
        spec = pl.BlockSpec((rb, cols), lambda h, i, nb=rows // rb: (jnp.minimum(h * n_tiles + i, nb - 1), 0))
        w_specs.append(spec)
        w_shapes.append(jax.ShapeDtypeStruct(w.shape, BF16))
    out = pl.pallas_call(
        _attn_prompt_kernel,
        grid=(n_heads, n_tiles),
        in_specs=[cur(0), prev(n_heads), cur(n_heads), prev(2 * n_heads), cur(2 * n_heads)] + w_specs,
        out_specs=[pl.BlockSpec((t, HEAD_DIM), lambda h, i: (i, h))] + w_specs,
        out_shape=[jax.ShapeDtypeStruct((s, n_heads * HEAD_DIM), BF16)] + w_shapes,
        scratch_shapes=[pltpu.VMEM((len(DILATIONS), t, HEAD_DIM), F32)] * 3,
        compiler_params=_params("arbitrary", "arbitrary"),
        name="attn_prompt",
    )(qkv, qkv, qkv, qkv, qkv, *weights)
    return out[0], tuple(out[1:])


def _attn_sample_kernel(qkv_ref, kt_ref, vt_ref, kf_ref, vf_ref, o_ref, kn_scr, vn_scr, *, n_heads):
    n_new = qkv_ref.shape[1]
    tail = kt_ref.shape[1]
    d_far = DILATIONS[-1]
    kn_scr[0:tail] = kt_ref[0]
    vn_scr[0:tail] = vt_ref[0]
    kn_scr[tail:tail + n_new] = qkv_ref[0, :, n_heads:2 * n_heads, :]
    vn_scr[tail:tail + n_new] = qkv_ref[0, :, 2 * n_heads:3 * n_heads, :]
    n_near_far = tail // d_far + 1

    for i in range(n_new):
        q = qkv_ref[0, i, 0:n_heads, :] * (HEAD_DIM ** -0.5)
        lists = []
        for d in DILATIONS[:-1]:
            rows = _rows(tail + i - d * SPAN, SPAN + 1, d)
            lists.append((kn_scr[rows], vn_scr[rows]))
        rows = _rows(tail + i - d_far * (n_near_far - 1), n_near_far, d_far)
        lists.append((kn_scr[rows], vn_scr[rows]))
        lists.append((kf_ref[0, :, i], vf_ref[0, :, i]))
        scores = [jnp.sum(k * q[None], axis=-1, keepdims=True) for k, _ in lists]
        m = functools.reduce(jnp.maximum, [jnp.max(s, axis=0) for s in scores])
        den = jnp.zeros((n_heads, 1), F32)
        num = jnp.zeros((n_heads, HEAD_DIM), F32)
        for s, (_, v) in zip(scores, lists):
            p = jnp.exp(s - m[None])
            den = den + jnp.sum(p, axis=0)
            num = num + jnp.sum(p * v, axis=0)
        o_ref[0, i] = num / den


def _attn_sample(qkv, cache_k, cache_v):
    b, n_new, _, _ = qkv.shape
    buf, n_heads = cache_k.shape[1], cache_k.shape[2]
    d_mid, d_far = DILATIONS[1], DILATIONS[2]
    tail = d_mid * SPAN
    assert buf == d_far * SPAN and n_new <= d_mid and tail % d_far == 0
    n_far = (buf - tail) // d_far
    far_shape = (b, buf // d_far, d_far, n_heads, HEAD_DIM)
    near = pl.BlockSpec((1, tail, n_heads, HEAD_DIM), lambda i: (i, buf // tail - 1, 0, 0))
    far = pl.BlockSpec((1, n_far, n_new, n_heads, HEAD_DIM), lambda i: (i, 0, 0, 0, 0))
    new = pl.BlockSpec((1, n_new, 3 * n_heads, HEAD_DIM), lambda i: (i, 0, 0, 0))
    ctx = pltpu.VMEM((tail + 8, n_heads, HEAD_DIM), F32)
    return pl.pallas_call(
        functools.partial(_attn_sample_kernel, n_heads=n_heads),
        grid=(b,),
        in_specs=[new, near, near, far, far],
        out_specs=pl.BlockSpec((1, n_new, n_heads, HEAD_DIM), lambda i: (i, 0, 0, 0)),
        out_shape=jax.ShapeDtypeStruct((b, n_new, n_heads, HEAD_DIM), F32),
        scratch_shapes=[ctx, ctx],
        compiler_params=_params("parallel"),
        name="attn_sample",
    )(qkv, cache_k, cache_v, cache_k.reshape(far_shape), cache_v.reshape(far_shape))


def _ln_swish(y, g, b):
    mu = jnp.mean(y, axis=-1, keepdims=True)
    yc = y - mu
    var = jnp.mean(yc * yc, axis=-1, keepdims=True)
    z = yc * lax.rsqrt(var + LN_EPS) * g + b
    return z * jax.nn.sigmoid(z)


def _conv_prompt_kernel(up_ref, u_ref, w_ref, b_ref, lg_ref, lb_ref, o_ref, ctx_scr, y_scr):
    tm, c = u_ref.shape
    n_slabs = c // LANES
    slab = lambda cs: slice(cs * LANES, (cs + 1) * LANES)

    @pl.when(pl.program_id(0) > 0)
    def _():
        for cs in range(n_slabs):
            ctx_scr[cs, 0:CONV_HALO, :] = up_ref[:, slab(cs)]

    @pl.when(pl.program_id(0) == 0)
    def _():
        for cs in range(n_slabs):
            ctx_scr[cs, 0:CONV_HALO, :] = jnp.zeros((CONV_HALO, LANES), F32)

    for cs in range(n_slabs):
        ctx_scr[cs, CONV_HALO:CONV_HALO + tm, :] = u_ref[:, slab(cs)]

    off = CONV_HALO - (CONV_K - 1)
    rc = 32

    for cs in range(n_slabs):
        taps = [jnp.broadcast_to(w_ref[j:j + 1, slab(cs)], (rc, LANES)) for j in range(CONV_K)]
        bias = jnp.broadcast_to(b_ref[:, slab(cs)], (rc, LANES))

        def chunk(i, carry, cs=cs, taps=taps, bias=bias):
            r0 = i * (2 * rc)
            for phase in range(2):
                acc = bias
                for j in range(CONV_K):
                    acc = acc + ctx_scr[cs, pl.ds(r0 + phase + off + j, rc, stride=2), :] * taps[j]
                y_scr[cs, pl.ds(r0 + phase, rc, stride=2), :] = acc
            return carry

        lax.fori_loop(0, tm // (2 * rc), chunk, 0)

    ln_rows = 16
    ln_unroll = 4

    def ln_rows_at(row0):
        rows = pl.ds(pl.multiple_of(row0, ln_rows), ln_rows)
        ys = [y_scr[cs, rows, :] for cs in range(n_slabs)]
        mu = jnp.sum(functools.reduce(jnp.add, ys), axis=-1, keepdims=True) * (1.0 / c)
        ycs = [y - mu for y in ys]
        var = jnp.sum(functools.reduce(jnp.add, [yc * yc for yc in ycs]), axis=-1, keepdims=True) * (1.0 / c)
        inv = lax.rsqrt(var + LN_EPS)
        for cs in range(n_slabs):
            z = ycs[cs] * inv * lg_ref[:, slab(cs)] + lb_ref[:, slab(cs)]
            o_ref[rows, slab(cs)] = (z * jax.nn.sigmoid(z)).astype(o_ref.dtype)

    def ln_chunk(i, carry):
        for j in range(ln_unroll):
            ln_rows_at((i * ln_unroll + j) * ln_rows)
        return carry

    lax.fori_loop(0, tm // (ln_rows * ln_unroll), ln_chunk, 0)


def _conv_prompt(u, conv_w, conv_b, ln_g, ln_b):
    s, c = u.shape
    tm = _tile(s, 256, CONV_HALO)
    vec = pl.BlockSpec((1, c), lambda i: (0, 0))
    return pl.pallas_call(
        _conv_prompt_kernel,
        grid=(s // tm,),
        in_specs=[pl.BlockSpec((CONV_HALO, c), lambda i: (jnp.maximum(i * (tm // CONV_HALO) - 1, 0), 0)),
                  pl.BlockSpec((tm, c), lambda i: (i, 0)),
                  pl.BlockSpec((CONV_K, c), lambda i: (0, 0)), vec, vec, vec],
        out_specs=pl.BlockSpec((tm, c), lambda i: (i, 0)),
        out_shape=jax.ShapeDtypeStruct((s, c), BF16),
        scratch_shapes=[pltpu.VMEM((c // LANES, CONV_HALO + tm, LANES), F32),
                        pltpu.VMEM((c // LANES, tm, LANES), F32)],
        compiler_params=_params("parallel"),
        name="conv_prompt",
    )(u, u, conv_w, conv_b, ln_g, ln_b)


def _conv_sample_kernel(st_ref, u_ref, w_ref, b_ref, lg_ref, lb_ref, o_ref, ns_ref, ctx_scr):
    n_state, n_new = st_ref.shape[1], u_ref.shape[1]
    ctx_scr[0:n_state, :] = st_ref[0]
    ctx_scr[n_state:n_state + n_new, :] = u_ref[0]
    acc = jnp.zeros(u_ref.shape[1:], F32)
    for j in range(CONV_K):
        acc = acc + ctx_scr[j:j + n_new, :] * w_ref[j:j + 1, :]
    o_ref[0] = _ln_swish(acc + b_ref[...], lg_ref[...], lb_ref[...])
    ns_ref[0] = ctx_scr[n_new:n_new + n_state, :]


def _conv_sample(state, u, conv_w, conv_b, ln_g, ln_b):
    b, n_state, c = state.shape
    n_new = u.shape[1]
    assert n_state == CONV_K - 1
    vec = pl.BlockSpec((1, c), lambda i: (0, 0))
    return pl.pallas_call(
        _conv_sample_kernel,
        grid=(b,),
        in_specs=[pl.BlockSpec((1, n_state, c), lambda i: (i, 0, 0)),
                  pl.BlockSpec((1, n_new, c), lambda i: (i, 0, 0)),
                  pl.BlockSpec((CONV_K, c), lambda i: (0, 0)), vec, vec, vec],
        out_specs=[pl.BlockSpec((1, n_new, c), lambda i: (i, 0, 0)),
                   pl.BlockSpec((1, n_state, c), lambda i: (i, 0, 0))],
        out_shape=[jax.ShapeDtypeStruct((b, n_new, c), F32),
                   jax.ShapeDtypeStruct((b, n_state, c), F32)],
        scratch_shapes=[pltpu.VMEM((n_state + n_new, c), F32)],
        compiler_params=_params("parallel"),
        name="conv_sample",
    )(state, u, conv_w, conv_b, ln_g, ln_b)


def _outproj_kernel(x_ref, a_ref, c_ref, wa_ref, wc_ref, o_ref, wa_bf, wc_bf):
    @pl.when(pl.program_id(1) == 0)
    def _():
        _cast_weight(wa_ref, wa_bf)
        _cast_weight(wc_ref, wc_bf)

    for r0, r1 in _row_parts(x_ref.shape[0]):
        acc = _dot(a_ref[r0:r1, :].astype(BF16), wa_bf[...]) + _dot(c_ref[r0:r1, :].astype(BF16), wc_bf[...])
        o_ref[r0:r1, :] = x_ref[r0:r1, :] + acc


def _outproj(x, attn, conv, w_out):
    m, d = x.shape
    a, c = attn.shape[1], conv.shape[1]
    assert a == c
    tm = _tile(m, 1024, 16)
    tn = _tile(d, 512, LANES)
    return pl.pallas_call(
        _outproj_kernel,
        grid=(d // tn, m // tm),
        in_specs=[pl.BlockSpec((tm, tn), lambda n, i: (i, n)),
                  pl.BlockSpec((tm, a), lambda n, i: (i, 0)),
                  pl.BlockSpec((tm, c), lambda n, i: (i, 0)),
                  pl.BlockSpec((a, tn), lambda n, i: (0, n)),
                  pl.BlockSpec((c, tn), lambda n, i: (1, n))],
        out_specs=pl.BlockSpec((tm, tn), lambda n, i: (i, n)),
        out_shape=jax.ShapeDtypeStruct((m, d), F32),
        scratch_shapes=[pltpu.VMEM((a, tn), BF16), pltpu.VMEM((c, tn), BF16)],
        compiler_params=_params("arbitrary", "arbitrary"),
        name="out_proj",
    )(x, attn, conv, w_out, w_out)


def _ffn_kernel(x_hbm, g2_ref, wg_ref, wu_ref, wd_ref, gf_ref, *rest, roll):
    if roll is None:
        y_hbm, acc_ref, h_ref, x_sem, y_sem = rest
    else:
        (ck_ref, cv_ref, new_ref, y_hbm, nk_ref, nv_ref,
         acc_ref, h_ref, x_sem, y_sem, slots, in_sem, out_sem, new_sem) = rest
    i, f = pl.program_id(0), pl.program_id(1)
    if roll is not None:
        step = i * pl.num_programs(1) + f
        _cache_roll_step(step, roll, (ck_ref, cv_ref), new_ref, (nk_ref, nv_ref), slots, in_sem, out_sem, new_sem)
    tm, d = acc_ref.shape
    rc = _tile(tm, NORM_ROWS)

    def tile_rows(t):
        return pl.ds(pl.multiple_of(t * tm, tm), tm)

    def y_copy(t):
        return pltpu.make_async_copy(acc_ref, y_hbm.at[tile_rows(t)], y_sem)

    def by_row_chunks(body):
        def step(c, carry):
            body(pl.ds(pl.multiple_of(c * rc, rc), rc))
            return carry
        lax.fori_loop(0, tm // rc, step, 0)

    @pl.when(f == 0)
    def _():
        @pl.when(i > 0)
        def _():
            y_copy(i - 1).wait()

        x_copy = pltpu.make_async_copy(x_hbm.at[tile_rows(i)], acc_ref, x_sem)
        x_copy.start()
        x_copy.wait()

        def body(rows):
            h_ref[rows, :] = _rms_to_bf16(acc_ref[rows, :], g2_ref[...])
        by_row_chunks(body)

    h = h_ref[...]
    gate = _dot(h, wg_ref[...])
    act = (gate * jax.nn.sigmoid(gate) * _dot(h, wu_ref[...])).astype(BF16)
    dc = _tile(d, FFN_DOWN_CHUNK, LANES)
    for c0 in range(0, d, dc):
        acc_ref[:, c0:c0 + dc] += _dot(act, wd_ref[:, c0:c0 + dc])

    @pl.when(f == pl.num_programs(1) - 1)
    def _():
        def body(rows):
            y = acc_ref[rows, :]
            ms = jnp.mean(y * y, axis=-1, keepdims=True)
            acc_ref[rows, :] = y * lax.rsqrt(ms + RMS_EPS) * gf_ref[...]
        by_row_chunks(body)
        y_copy(i).start()

        @pl.when(i == pl.num_programs(0) - 1)
        def _():
            y_copy(i).wait()


def _cache_roll_step(step, roll, caches, new_ref, rolled, slots, in_sem, out_sem, new_sem):
    n_chunks, per_seq, n_heads = roll
    rows = slots.shape[2]
    n_new = new_ref.shape[1]

    def chunk_copies(c):
        seq, j, slot = c // per_seq, c % per_seq, c % 2
        pairs = []
        for i in range(2):
            buf = slots.at[slot, i]
            pairs.append((
                pltpu.make_async_copy(caches[i].at[seq, pl.ds(j * rows + n_new, rows)], buf, in_sem.at[slot, i]),
                pltpu.make_async_copy(buf, rolled[i].at[seq, pl.ds(j * rows, rows)], out_sem.at[slot, i])))
        return pairs

    def new_copies(c):
        seq = c // per_seq
        return [pltpu.make_async_copy(new_ref.at[seq, :, pl.ds((i + 1) * n_heads, n_heads)],
                                      rolled[i].at[seq, pl.ds(per_seq * rows, n_new)], new_sem.at[i])
                for i in range(2)]

    @pl.when((step >= 2) & (step < n_chunks + 2))
    def _():
        for _, out in chunk_copies(step - 2):
            out.wait()

    @pl.when(step < n_chunks)
    def _():
        for inp, _ in chunk_copies(step):
            inp.start()

    @pl.when((step >= 1) & (step < n_chunks + 1))
    def _():
        for inp, out in chunk_copies(step - 1):
            inp.wait()
            out.start()

    @pl.when((step >= 1) & (step < n_chunks + 1) & ((step - 1) % per_seq == 0))
    def _():
        for cp in new_copies(step - 1):
            cp.wait()

    @pl.when((step < n_chunks) & (step % per_seq == 0))
    def _():
        for cp in new_copies(step):
            cp.start()


def _ffn(x, g2, w_gate, w_up, w_down, gf, caches=None):
    m, d = x.shape
    dff = w_gate.shape[1]
    tm = _tile(m, 1024, 16)
    tf = _tile(dff, 256, LANES)
    nf = dff // tf
    grid = (m // tm, nf)
    vec = pl.BlockSpec((1, d), lambda i, f: (0, 0))
    any_spec = pl.BlockSpec(memory_space=pl.ANY)
    in_specs = [any_spec, vec,
                pl.BlockSpec((d, tf), lambda i, f: (0, f)),
                pl.BlockSpec((d, tf), lambda i, f: (0, f)),
                pl.BlockSpec((tf, d), lambda i, f: (f, 0)), vec]
    out_specs = [any_spec]
    out_shape = [jax.ShapeDtypeStruct((m, d), F32)]
    args = [x, g2, w_gate, w_up, w_down, gf]
    roll = None
    scratch = [pltpu.VMEM((tm, d), F32), pltpu.VMEM((tm, d), BF16),
               pltpu.SemaphoreType.DMA(()), pltpu.SemaphoreType.DMA(())]
    if caches is not None:
        cache_k, cache_v, qkv_new = caches
        b, buf, n_heads, _ = cache_k.shape
        n_new = qkv_new.shape[1]
        assert buf == MAX_WINDOW, "the rolled cache keeps exactly the window"
        old = buf - n_new
        steps = grid[0] * grid[1]
        per_seq = max(p for p in range(2, old + 1) if old % p == 0 and b * p + 2 <= steps)
        n_chunks = b * per_seq
        in_specs += [any_spec] * 3
        out_specs += [any_spec] * 2
        out_shape += [jax.ShapeDtypeStruct((b, buf, n_heads, HEAD_DIM), F32)] * 2
        args += [cache_k, cache_v, qkv_new]
        scratch += [pltpu.VMEM((2, 2, old // per_seq, n_heads, HEAD_DIM), F32),
                    pltpu.SemaphoreType.DMA((2, 2)), pltpu.SemaphoreType.DMA((2, 2)),
                    pltpu.SemaphoreType.DMA((2,))]
        roll = (n_chunks, per_seq, n_heads)
    out = pl.pallas_call(
        functools.partial(_ffn_kernel, roll=roll),
        grid=grid,
        in_specs=in_specs,
        out_specs=out_specs,
        out_shape=out_shape,
        scratch_shapes=scratch,
        compiler_params=_params("arbitrary", "arbitrary"),
        name="ffn",
    )(*args)
    return out[0] if caches is None else out


def kernel(x_prompt, x_sample, cache_k, cache_v, state_conv, norm1_g, w_in, conv_w, conv_b, conv_ln_g,
           conv_ln_b, w_out, norm2_g, w_gate, w_up, w_down, final_g):
    depth = w_in.shape[0]
    bp, s, d = x_prompt.shape
    bs, t_new, _ = x_sample.shape
    assert depth == 1 and bp == 1, "one layer, one prompt sequence"
    attn_w = d // 2
    conv_c = d - attn_w
    n_heads = attn_w // HEAD_DIM
    buf = cache_k.shape[2]

    row = lambda v: v.reshape(1, -1)
    l = 0
    w_in_b, w_out_b = w_in[l], w_out[l]
    g1, g2, gf = row(norm1_g[l]), row(norm2_g[l]), row(final_g)
    cb, lg, lb = row(conv_b[l]), row(conv_ln_g[l]), row(conv_ln_b[l])

    ms = bs * t_new
    xs = x_sample.reshape(ms, d)
    pos_s = PAST_LEN + jnp.tile(jnp.arange(t_new, dtype=jnp.int32), bs)
    qkv_s, u_s = _project_in(xs, g1, w_in_b, pos_s, attn_w, conv_c)
    qkv_s = qkv_s.reshape(bs, t_new, 3 * n_heads, HEAD_DIM)

    xp = x_prompt.reshape(s, d)
    qkv_p, u_p = _project_in(xp, g1, w_in_b, jnp.arange(s, dtype=jnp.int32), attn_w, conv_c)
    attn_p, (w_gate_b, w_up_b, w_down_b) = _attn_prompt(qkv_p, n_heads, (w_gate[l], w_up[l], w_down[l]))
    conv_p = _conv_prompt(u_p, conv_w[l], cb, lg, lb)
    x1_p = _outproj(xp, attn_p, conv_p, w_out_b)
    y_p, nk_s, nv_s = _ffn(x1_p, g2, w_gate_b, w_up_b, w_down_b, gf, caches=(cache_k[l], cache_v[l], qkv_s))
    keep_p = min(MAX_WINDOW, s)
    nk_p = qkv_p[s - keep_p:, attn_w:2 * attn_w].reshape(1, 1, keep_p, n_heads, HEAD_DIM)
    nv_p = qkv_p[s - keep_p:, 2 * attn_w:].reshape(1, 1, keep_p, n_heads, HEAD_DIM)
    n_state = CONV_K - 1
    assert s >= n_state
    nc_p = u_p[s - n_state:].reshape(1, 1, n_state, conv_c)

    attn_s = _attn_sample(qkv_s, cache_k[l], cache_v[l])
    conv_s, nc_s = _conv_sample(state_conv[l], u_s.reshape(bs, t_new, conv_c), conv_w[l], cb, lg, lb)
    x1_s = _outproj(xs, attn_s.reshape(ms, attn_w), conv_s.reshape(ms, conv_c), w_out_b)
    y_s = _ffn(x1_s, g2, w_gate_b, w_up_b, w_down_b, gf)

    return (y_p.reshape(1, s, d), y_s.reshape(bs, t_new, d), nk_p, nv_p, nc_p,
            nk_s[None], nv_s[None], nc_s.reshape(1, bs, n_state, conv_c))
```

```python
import functools
import math

import jax
import jax.numpy as jnp
from jax import lax
from jax.experimental import pallas as pl
from jax.experimental.pallas import tpu as pltpu

F32 = jnp.float32
BF16 = jnp.bfloat16

HEAD_DIM = 128
ROPE_DIM = HEAD_DIM // 4
ROPE_THETA = 500000.0
CONV_K = 31
RMS_EPS = 1e-6
LN_EPS = 1e-5
PAST_LEN = 8192
DILATIONS = (1, 4, 16)
SPAN = 128
MAX_WINDOW = 2048
ATTN_TILE = MAX_WINDOW
ATTN_UNROLL = 8
CONV_HALO = 32
FFN_DOWN_CHUNK = 1024
NORM_ROWS = 64
FFN_TRANSFER_PARTS = 4
WEIGHT_CAST_ROWS = 512

LANES = 128
VMEM_LIMIT_BYTES = 56 * 1024 * 1024


def _params(*sem):
    return pltpu.CompilerParams(dimension_semantics=sem, vmem_limit_bytes=VMEM_LIMIT_BYTES)


def _tile(n, pref, mult=8):
    if n <= pref:
        return n
    for t in range(pref - pref % mult, 0, -mult):
        if n % t == 0:
            return t
    raise ValueError(f"no tile for {n}")


def _rms_to_bf16(x, g):
    ms = jnp.mean(x * x, axis=-1, keepdims=True)
    return (x * lax.rsqrt(ms + RMS_EPS) * g).astype(BF16)


def _dot(a, b):
    return jnp.dot(a, b, preferred_element_type=F32)


def _dot_nt(a, b):
    return lax.dot_general(a, b, (((1,), (1,)), ((), ())), preferred_element_type=F32)


def _norm_kernel(x_ref, g_ref, h_ref):
    h_ref[...] = _rms_to_bf16(x_ref[...], g_ref[...])


def _cast_weight(w_ref, w_bf):
    k = w_ref.shape[0]
    rc = _tile(k, WEIGHT_CAST_ROWS)
    for r0 in range(0, k, rc):
        w_bf[r0:r0 + rc, :] = w_ref[r0:r0 + rc, :].astype(BF16)


def _row_parts(tm):
    half = tm // 2
    return ((0, tm),) if half % 16 else ((0, half), (half, tm))


def _qkv_kernel(h_ref, w_ref, tab_ref, o_ref, w_bf):
    @pl.when(pl.program_id(1) == 0)
    def _():
        _cast_weight(w_ref, w_bf)

    for r0, r1 in _row_parts(h_ref.shape[0]):
        z = _dot(h_ref[r0:r1, :], w_bf[...])
        c, s1, s2 = tab_ref[0, 0, r0:r1, :], tab_ref[0, 1, r0:r1, :], tab_ref[0, 2, r0:r1, :]
        for j in range(z.shape[1] // HEAD_DIM):
            zj = z[:, j * HEAD_DIM:(j + 1) * HEAD_DIM]
            lo = pltpu.roll(zj, ROPE_DIM // 2, 1)
            hi = pltpu.roll(zj, HEAD_DIM - ROPE_DIM // 2, 1)
            o_ref[r0:r1, j * HEAD_DIM:(j + 1) * HEAD_DIM] = zj * c + lo * s1 + hi * s2


def _glu_kernel(h_ref, wa_ref, wg_ref, u_ref, wa_bf, wg_bf):
    @pl.when(pl.program_id(1) == 0)
    def _():
        _cast_weight(wa_ref, wa_bf)
        _cast_weight(wg_ref, wg_bf)

    for r0, r1 in _row_parts(h_ref.shape[0]):
        h = h_ref[r0:r1, :]
        u_ref[r0:r1, :] = _dot(h, wa_bf[...]) * jax.nn.sigmoid(_dot(h, wg_bf[...]))


def _rope_tables(pos):
    half = ROPE_DIM // 2
    inv = ROPE_THETA ** (-jnp.arange(half, dtype=F32) / half)
    ang = pos.astype(F32)[:, None] * inv[None, :]
    cos, sin = jnp.cos(ang), jnp.sin(ang)
    m = pos.shape[0]
    one = jnp.ones((m, HEAD_DIM - ROPE_DIM), F32)
    zero = jnp.zeros((m, HEAD_DIM - ROPE_DIM), F32)
    zh = jnp.zeros((m, half), F32)
    c = jnp.concatenate([cos, cos, one], axis=1)
    s1 = jnp.concatenate([zh, sin, zero], axis=1)
    s2 = jnp.concatenate([-sin, zh, zero], axis=1)
    rot = jnp.stack([c, s1, s2])
    ident = jnp.stack([jnp.ones_like(c), jnp.zeros_like(c), jnp.zeros_like(c)])
    return jnp.stack([rot, ident])


def _project_in(x, g1, w_in, pos, attn_w, conv_c):
    m, d = x.shape
    tr = _tile(m, 256)
    h = pl.pallas_call(
        _norm_kernel,
        grid=(m // tr,),
        in_specs=[pl.BlockSpec((tr, d), lambda i: (i, 0)), pl.BlockSpec((1, d), lambda i: (0, 0))],
        out_specs=pl.BlockSpec((tr, d), lambda i: (i, 0)),
        out_shape=jax.ShapeDtypeStruct((m, d), BF16),
        compiler_params=_params("parallel"),
        name="norm1",
    )(x, g1)

    tm = _tile(m, 1024, 16)
    tn = _tile(attn_w, 1024, LANES)
    once = pl.Buffered(1) if m // tm > 1 else None
    n_rope = 2 * attn_w // tn
    qkv = pl.pallas_call(
        _qkv_kernel,
        grid=(3 * attn_w // tn, m // tm),
        in_specs=[pl.BlockSpec((tm, d), lambda n, i: (i, 0)),
                  pl.BlockSpec((d, tn), lambda n, i: (0, n), pipeline_mode=once),
                  pl.BlockSpec((1, 3, tm, HEAD_DIM), lambda n, i: (jnp.where(n < n_rope, 0, 1), 0, i, 0))],
        out_specs=pl.BlockSpec((tm, tn), lambda n, i: (i, n)),
        out_shape=jax.ShapeDtypeStruct((m, 3 * attn_w), F32),
        scratch_shapes=[pltpu.VMEM((d, tn), BF16)],
        compiler_params=_params("arbitrary", "arbitrary"),
        name="qkv_proj",
    )(h, w_in, _rope_tables(pos))

    tc = _tile(conv_c, 512, LANES)
    a0 = 3 * attn_w // tc
    g0 = (3 * attn_w + conv_c) // tc
    u = pl.pallas_call(
        _glu_kernel,
        grid=(conv_c // tc, m // tm),
        in_specs=[pl.BlockSpec((tm, d), lambda n, i: (i, 0)),
                  pl.BlockSpec((d, tc), lambda n, i: (0, a0 + n), pipeline_mode=once),
                  pl.BlockSpec((d, tc), lambda n, i: (0, g0 + n), pipeline_mode=once)],
        out_specs=pl.BlockSpec((tm, tc), lambda n, i: (i, n)),
        out_shape=jax.ShapeDtypeStruct((m, conv_c), F32),
        scratch_shapes=[pltpu.VMEM((d, tc), BF16)] * 2,
        compiler_params=_params("arbitrary", "arbitrary"),
        name="glu_proj",
    )(h, w_in, w_in)
    return qkv, u


def _rows(start, size, stride):
    return pl.ds(start, size) if stride == 1 else pl.ds(start, size, stride=stride)


def _attn_prompt_kernel(q_ref, kp_ref, kc_ref, vp_ref, vc_ref, *rest):
    n_w = (len(rest) - 4) // 2
    w_in, o_ref, w_out = rest[:n_w], rest[n_w], rest[n_w + 1:2 * n_w + 1]
    num_scr, m_scr, l_scr = rest[2 * n_w + 1:]
    for src, dst in zip(w_in, w_out):
        rc = _tile(src.shape[0], 16, 16)
        for r0 in range(0, src.shape[0], rc):
            dst[r0:r0 + rc, :] = src[r0:r0 + rc, :].astype(BF16)

    tile = pl.program_id(1)
    t = q_ref.shape[0]
    scale = HEAD_DIM ** -0.5
    qi = lax.broadcasted_iota(jnp.int32, (SPAN, 2 * SPAN), 0)
    kj = lax.broadcasted_iota(jnp.int32, (SPAN, 2 * SPAN), 1)
    band = (kj >= qi) & (kj <= qi + SPAN)
    first_key = jnp.where(tile > 0, 0, SPAN)
    band_first = band & (kj >= first_key)

    def attend(gi, d, q_start, q, k, v, mask):
        s = _dot_nt(q.astype(BF16), k.astype(BF16)) * scale
        s = jnp.where(mask, s, -jnp.inf)
        m = jnp.max(s, axis=-1, keepdims=True)
        p = jnp.exp(s - m)
        l = jnp.sum(p, axis=-1, keepdims=True)
        num = _dot(p.astype(BF16), v.astype(BF16))
        rows = _rows(q_start, SPAN, d)
        num_scr[gi, rows, :] = num
        m_scr[gi, rows, :] = jnp.broadcast_to(m, (SPAN, HEAD_DIM))
        l_scr[gi, rows, :] = jnp.broadcast_to(l, (SPAN, HEAD_DIM))

    def first_block(gi, d, r):
        q = q_ref[_rows(r, SPAN, d), :]
        k = jnp.concatenate([kp_ref[_rows(t - d * SPAN + r, SPAN, d), :],
                             kc_ref[_rows(r, SPAN, d), :]], axis=0)
        v = jnp.concatenate([vp_ref[_rows(t - d * SPAN + r, SPAN, d), :],
                             vc_ref[_rows(r, SPAN, d), :]], axis=0)
        attend(gi, d, r, q, k, v, band_first)

    def later_block(gi, d, r, b):
        q_start = r + d * SPAN * b
        if d == 1:
            q_start = pl.multiple_of(q_start, SPAN)
        k_rows = _rows(q_start - d * SPAN, 2 * SPAN, d)
        attend(gi, d, q_start, q_ref[_rows(q_start, SPAN, d), :], kc_ref[k_rows, :], vc_ref[k_rows, :], band)

    u = ATTN_UNROLL
    for gi, d in enumerate(DILATIONS):
        n_blocks = t // (d * SPAN)
        if n_blocks >= u:
            assert n_blocks % u == 0

            def class_body(r, carry, gi=gi, d=d, n_blocks=n_blocks):
                first_block(gi, d, r)
                for b in range(1, u):
                    later_block(gi, d, r, b)

                def rest(i, c):
                    for j in range(u):
                        later_block(gi, d, r, i * u + j)
                    return c

                if n_blocks > u:
                    lax.fori_loop(1, n_blocks // u, rest, 0)
                return carry

            if d == 1:
                class_body(0, 0)
            else:
                lax.fori_loop(0, d, class_body, 0)
        else:
            per_iter = u // n_blocks
            assert u % n_blocks == 0 and d % per_iter == 0

            def class_group(i, carry, gi=gi, d=d, n_blocks=n_blocks, per_iter=per_iter):
                for j in range(per_iter):
                    r = i * per_iter + j
                    first_block(gi, d, r)
                    for b in range(1, n_blocks):
                        later_block(gi, d, r, b)
                return carry

            lax.fori_loop(0, d // per_iter, class_group, 0)

    def combine(c, carry):
        rows = pl.ds(pl.multiple_of(c * SPAN, SPAN), SPAN)
        ms = [m_scr[g, rows, :] for g in range(len(DILATIONS))]
        m_all = functools.reduce(jnp.maximum, ms)
        num = jnp.zeros((SPAN, HEAD_DIM), F32)
        den = jnp.zeros((SPAN, HEAD_DIM), F32)
        for g in range(len(DILATIONS)):
            sc = jnp.exp(ms[g] - m_all)
            num = num + num_scr[g, rows, :] * sc
            den = den + l_scr[g, rows, :] * sc
        o_ref[rows, :] = (num / den).astype(o_ref.dtype)
        return carry

    lax.fori_loop(0, t // SPAN, combine, 0)


def _attn_prompt(qkv, n_heads, weights=()):
    s = qkv.shape[0]
    t = ATTN_TILE
    assert s % t == 0, "prompt length must be a multiple of the dilation tile"
    n_tiles = s // t
    cur = lambda off: pl.BlockSpec((t, HEAD_DIM), lambda h, i: (i, off + h))
    prev = lambda off: pl.BlockSpec((t, HEAD_DIM), lambda h, i: (jnp.maximum(i - 1, 0), off + h))
    w_specs, w_shapes = [], []
    for w in weights:
        rows, cols = w.shape
        rb = next(r for r in range(16, rows + 1, 16) if rows % r == 0 and rows // r <= n_heads * n_tiles)
        spec = pl.BlockSpec((rb, cols), lambda h, i, nb=rows // rb: (jnp.minimum(h * n_tiles + i, nb - 1), 0))
        w_specs.append(spec)
        w_shapes.append(jax.ShapeDtypeStruct(w.shape, BF16))
    out = pl.pallas_call(
        _attn_prompt_kernel,
        grid=(n_heads, n_tiles),
        in_specs=[cur(0), prev(n_heads), cur(n_heads), prev(2 * n_heads), cur(2 * n_heads)] + w_specs,
        out_specs=[pl.BlockSpec((t, HEAD_DIM), lambda h, i: (i, h))] + w_specs,
        out_shape=[jax.ShapeDtypeStruct((s, n_heads * HEAD_DIM), BF16)] + w_shapes,
        scratch_shapes=[pltpu.VMEM((len(DILATIONS), t, HEAD_DIM), F32)] * 3,
        compiler_params=_params("arbitrary", "arbitrary"),
        name="attn_prompt",
    )(qkv, qkv, qkv, qkv, qkv, *weights)
    return out[0], tuple(out[1:])


def _attn_sample_kernel(qkv_ref, kt_ref, vt_ref, kf_ref, vf_ref, o_ref, kn_scr, vn_scr, *, n_heads):
    n_new = qkv_ref.shape[1]
    tail = kt_ref.shape[1]
    d_far = DILATIONS[-1]
    kn_scr[0:tail] = kt_ref[0]
    vn_scr[0:tail] = vt_ref[0]
    kn_scr[tail:tail + n_new] = qkv_ref[0, :, n_heads:2 * n_heads, :]
    vn_scr[tail:tail + n_new] = qkv_ref[0, :, 2 * n_heads:3 * n_heads, :]
    n_near_far = tail // d_far + 1

    for i in range(n_new):
        q = qkv_ref[0, i, 0:n_heads, :] * (HEAD_DIM ** -0.5)
        lists = []
        for d in DILATIONS[:-1]:
            rows = _rows(tail + i - d * SPAN, SPAN + 1, d)
            lists.append((kn_scr[rows], vn_scr[rows]))
        rows = _rows(tail + i - d_far * (n_near_far - 1), n_near_far, d_far)
        lists.append((kn_scr[rows], vn_scr[rows]))
        lists.append((kf_ref[0, :, i], vf_ref[0, :, i]))
        scores = [jnp.sum(k * q[None], axis=-1, keepdims=True) for k, _ in lists]
        m = functools.reduce(jnp.maximum, [jnp.max(s, axis=0) for s in scores])
        den = jnp.zeros((n_heads, 1), F32)
        num = jnp.zeros((n_heads, HEAD_DIM), F32)
        for s, (_, v) in zip(scores, lists):
            p = jnp.exp(s - m[None])
            den = den + jnp.sum(p, axis=0)
            num = num + jnp.sum(p * v, axis=0)
        o_ref[0, i] = num / den


def _attn_sample(qkv, cache_k, cache_v):
    b, n_new, _, _ = qkv.shape
    buf, n_heads = cache_k.shape[1], cache_k.shape[2]
    d_mid, d_far = DILATIONS[1], DILATIONS[2]
    tail = d_mid * SPAN
    assert buf == d_far * SPAN and n_new <= d_mid and tail % d_far == 0
    n_far = (buf - tail) // d_far
    far_shape = (b, buf // d_far, d_far, n_heads, HEAD_DIM)
    near = pl.BlockSpec((1, tail, n_heads, HEAD_DIM), lambda i: (i, buf // tail - 1, 0, 0))
    far = pl.BlockSpec((1, n_far, n_new, n_heads, HEAD_DIM), lambda i: (i, 0, 0, 0, 0))
    new = pl.BlockSpec((1, n_new, 3 * n_heads, HEAD_DIM), lambda i: (i, 0, 0, 0))
    ctx = pltpu.VMEM((tail + 8, n_heads, HEAD_DIM), F32)
    return pl.pallas_call(
        functools.partial(_attn_sample_kernel, n_heads=n_heads),
        grid=(b,),
        in_specs=[new, near, near, far, far],
        out_specs=pl.BlockSpec((1, n_new, n_heads, HEAD_DIM), lambda i: (i, 0, 0, 0)),
        out_shape=jax.ShapeDtypeStruct((b, n_new, n_heads, HEAD_DIM), F32),
        scratch_shapes=[ctx, ctx],
        compiler_params=_params("parallel"),
        name="attn_sample",
    )(qkv, cache_k, cache_v, cache_k.reshape(far_shape), cache_v.reshape(far_shape))


def _ln_swish(y, g, b):
    mu = jnp.mean(y, axis=-1, keepdims=True)
    yc = y - mu
    var = jnp.mean(yc * yc, axis=-1, keepdims=True)
    z = yc * lax.rsqrt(var + LN_EPS) * g + b
    return z * jax.nn.sigmoid(z)


def _conv_prompt_kernel(up_ref, u_ref, w_ref, b_ref, lg_ref, lb_ref, o_ref, ctx_scr, y_scr):
    tm, c = u_ref.shape
    n_slabs = c // LANES
    slab = lambda cs: slice(cs * LANES, (cs + 1) * LANES)

    @pl.when(pl.program_id(0) > 0)
    def _():
        for cs in range(n_slabs):
            ctx_scr[cs, 0:CONV_HALO, :] = up_ref[:, slab(cs)]

    @pl.when(pl.program_id(0) == 0)
    def _():
        for cs in range(n_slabs):
            ctx_scr[cs, 0:CONV_HALO, :] = jnp.zeros((CONV_HALO, LANES), F32)

    for cs in range(n_slabs):
        ctx_scr[cs, CONV_HALO:CONV_HALO + tm, :] = u_ref[:, slab(cs)]

    off = CONV_HALO - (CONV_K - 1)
    rc = 32

    for cs in range(n_slabs):
        taps = [jnp.broadcast_to(w_ref[j:j + 1, slab(cs)], (rc, LANES)) for j in range(CONV_K)]
        bias = jnp.broadcast_to(b_ref[:, slab(cs)], (rc, LANES))

        def chunk(i, carry, cs=cs, taps=taps, bias=bias):
            r0 = i * (2 * rc)
            for phase in range(2):
                acc = bias
                for j in range(CONV_K):
                    acc = acc + ctx_scr[cs, pl.ds(r0 + phase + off + j, rc, stride=2), :] * taps[j]
                y_scr[cs, pl.ds(r0 + phase, rc, stride=2), :] = acc
            return carry

        lax.fori_loop(0, tm // (2 * rc), chunk, 0)

    ln_rows = 16
    ln_unroll = 4

    def ln_rows_at(row0):
        rows = pl.ds(pl.multiple_of(row0, ln_rows), ln_rows)
        ys = [y_scr[cs, rows, :] for cs in range(n_slabs)]
        mu = jnp.sum(functools.reduce(jnp.add, ys), axis=-1, keepdims=True) * (1.0 / c)
        ycs = [y - mu for y in ys]
        var = jnp.sum(functools.reduce(jnp.add, [yc * yc for yc in ycs]), axis=-1, keepdims=True) * (1.0 / c)
        inv = lax.rsqrt(var + LN_EPS)
        for cs in range(n_slabs):
            z = ycs[cs] * inv * lg_ref[:, slab(cs)] + lb_ref[:, slab(cs)]
            o_ref[rows, slab(cs)] = (z * jax.nn.sigmoid(z)).astype(o_ref.dtype)

    def ln_chunk(i, carry):
        for j in range(ln_unroll):
            ln_rows_at((i * ln_unroll + j) * ln_rows)
        return carry

    lax.fori_loop(0, tm // (ln_rows * ln_unroll), ln_chunk, 0)


def _conv_prompt(u, conv_w, conv_b, ln_g, ln_b):
    s, c = u.shape
    tm = _tile(s, 256, CONV_HALO)
    vec = pl.BlockSpec((1, c), lambda i: (0, 0))
    return pl.pallas_call(
        _conv_prompt_kernel,
        grid=(s // tm,),
        in_specs=[pl.BlockSpec((CONV_HALO, c), lambda i: (jnp.maximum(i * (tm // CONV_HALO) - 1, 0), 0)),
                  pl.BlockSpec((tm, c), lambda i: (i, 0)),
                  pl.BlockSpec((CONV_K, c), lambda i: (0, 0)), vec, vec, vec],
        out_specs=pl.BlockSpec((tm, c), lambda i: (i, 0)),
        out_shape=jax.ShapeDtypeStruct((s, c), BF16),
        scratch_shapes=[pltpu.VMEM((c // LANES, CONV_HALO + tm, LANES), F32),
                        pltpu.VMEM((c // LANES, tm, LANES), F32)],
        compiler_params=_params("parallel"),
        name="conv_prompt",
    )(u, u, conv_w, conv_b, ln_g, ln_b)


def _conv_sample_kernel(st_ref, u_ref, w_ref, b_ref, lg_ref, lb_ref, o_ref, ns_ref, ctx_scr):
    n_state, n_new = st_ref.shape[1], u_ref.shape[1]
    ctx_scr[0:n_state, :] = st_ref[0]
    ctx_scr[n_state:n_state + n_new, :] = u_ref[0]
    acc = jnp.zeros(u_ref.shape[1:], F32)
    for j in range(CONV_K):
        acc = acc + ctx_scr[j:j + n_new, :] * w_ref[j:j + 1, :]
    o_ref[0] = _ln_swish(acc + b_ref[...], lg_ref[...], lb_ref[...])
    ns_ref[0] = ctx_scr[n_new:n_new + n_state, :]


def _conv_sample(state, u, conv_w, conv_b, ln_g, ln_b):
    b, n_state, c = state.shape
    n_new = u.shape[1]
    assert n_state == CONV_K - 1
    vec = pl.BlockSpec((1, c), lambda i: (0, 0))
    return pl.pallas_call(
        _conv_sample_kernel,
        grid=(b,),
        in_specs=[pl.BlockSpec((1, n_state, c), lambda i: (i, 0, 0)),
                  pl.BlockSpec((1, n_new, c), lambda i: (i, 0, 0)),
                  pl.BlockSpec((CONV_K, c), lambda i: (0, 0)), vec, vec, vec],
        out_specs=[pl.BlockSpec((1, n_new, c), lambda i: (i, 0, 0)),
                   pl.BlockSpec((1, n_state, c), lambda i: (i, 0, 0))],
        out_shape=[jax.ShapeDtypeStruct((b, n_new, c), F32),
                   jax.ShapeDtypeStruct((b, n_state, c), F32)],
        scratch_shapes=[pltpu.VMEM((n_state + n_new, c), F32)],
        compiler_params=_params("parallel"),
        name="conv_sample",
    )(state, u, conv_w, conv_b, ln_g, ln_b)


def _outproj_kernel(x_ref, a_ref, c_ref, wa_ref, wc_ref, o_ref, wa_bf, wc_bf):
    @pl.when(pl.program_id(1) == 0)
    def _():
        _cast_weight(wa_ref, wa_bf)
        _cast_weight(wc_ref, wc_bf)

    for r0, r1 in _row_parts(x_ref.shape[0]):
        acc = _dot(a_ref[r0:r1, :].astype(BF16), wa_bf[...]) + _dot(c_ref[r0:r1, :].astype(BF16), wc_bf[...])
        o_ref[r0:r1, :] = x_ref[r0:r1, :] + acc


def _outproj(x, attn, conv, w_out):
    m, d = x.shape
    a, c = attn.shape[1], conv.shape[1]
    assert a == c
    tm = _tile(m, 1024, 16)
    tn = _tile(d, 512, LANES)
    return pl.pallas_call(
        _outproj_kernel,
        grid=(d // tn, m // tm),
        in_specs=[pl.BlockSpec((tm, tn), lambda n, i: (i, n)),
                  pl.BlockSpec((tm, a), lambda n, i: (i, 0)),
                  pl.BlockSpec((tm, c), lambda n, i: (i, 0)),
                  pl.BlockSpec((a, tn), lambda n, i: (0, n)),
                  pl.BlockSpec((c, tn), lambda n, i: (1, n))],
        out_specs=pl.BlockSpec((tm, tn), lambda n, i: (i, n)),
        out_shape=jax.ShapeDtypeStruct((m, d), F32),
        scratch_shapes=[pltpu.VMEM((a, tn), BF16), pltpu.VMEM((c, tn), BF16)],
        compiler_params=_params("arbitrary", "arbitrary"),
        name="out_proj",
    )(x, attn, conv, w_out, w_out)


def _ffn_kernel(*refs, roll, rider):
    refs = list(refs)
    take = lambda n: [refs.pop(0) for _ in range(n)]
    x_hbm, g2_ref, wg_ref, wu_ref, wd_ref, gf_ref = take(6)
    xs_hbm, = take(1) if rider else (None,)
    caches = take(3) if roll else None
    y_hbm, = take(1)
    ys_hbm, = take(1) if rider else (None,)
    rolled = take(2) if roll else None
    acc_ref, h_ref, x_sem, y_sem = take(4)
    acc_s, h_s, s_sem = take(3) if rider else (None, None, None)
    i, f = pl.program_id(0), pl.program_id(1)
    last_f = pl.num_programs(1) - 1
    if roll:
        slots, in_sem, out_sem, new_sem = take(4)
        _cache_roll_step(i * pl.num_programs(1) + f, roll, caches[:2], caches[2], rolled,
                         slots, in_sem, out_sem, new_sem)
    tm, d = acc_ref.shape
    parts = FFN_TRANSFER_PARTS if tm % (16 * FFN_TRANSFER_PARTS) == 0 else 1
    tp = tm // parts

    def x_copy(t, c):
        return pltpu.make_async_copy(x_hbm.at[pl.ds(pl.multiple_of(t * tm + c * tp, tp), tp)],
                                     acc_ref.at[pl.ds(c * tp, tp)], x_sem.at[c])

    def y_copy(t, c):
        return pltpu.make_async_copy(acc_ref.at[pl.ds(c * tp, tp)],
                                     y_hbm.at[pl.ds(pl.multiple_of(t * tm + c * tp, tp), tp)], y_sem.at[c])

    def by_row_chunks(start, size, body):
        rc = _tile(size, NORM_ROWS)

        def step(c, carry):
            body(pl.ds(pl.multiple_of(start + c * rc, rc), rc))
            return carry
        lax.fori_loop(0, size // rc, step, 0)

    def norm_in(acc, h):
        def body(rows):
            h[rows, :] = _rms_to_bf16(acc[rows, :], g2_ref[...])
        return body

    def norm_out(acc):
        def body(rows):
            y = acc[rows, :]
            ms = jnp.mean(y * y, axis=-1, keepdims=True)
            acc[rows, :] = y * lax.rsqrt(ms + RMS_EPS) * gf_ref[...]
        return body

    def swiglu_into(acc, h_rows):
        h = h_rows[...]
        gate = _dot(h, wg_ref[...])
        act = (gate * jax.nn.sigmoid(gate) * _dot(h, wu_ref[...])).astype(BF16)
        dc = _tile(d, FFN_DOWN_CHUNK, LANES)
        for c0 in range(0, d, dc):
            acc[:, c0:c0 + dc] += _dot(act, wd_ref[:, c0:c0 + dc])

    @pl.when(f == 0)
    def _():
        for c in range(parts):
            @pl.when(i > 0)
            def _(c=c):
                y_copy(i - 1, c).wait()
            x_copy(i, c).start()
        for c in range(parts):
            x_copy(i, c).wait()
            by_row_chunks(c * tp, tp, norm_in(acc_ref, h_ref))
        if rider:
            @pl.when(i == 0)
            def _():
                cp = pltpu.make_async_copy(xs_hbm, acc_s, s_sem.at[0])
                cp.start()
                cp.wait()
                by_row_chunks(0, acc_s.shape[0], norm_in(acc_s, h_s))

    swiglu_into(acc_ref, h_ref)
    if rider:
        @pl.when(i == 0)
        def _():
            swiglu_into(acc_s, h_s)

    @pl.when(f == last_f)
    def _():
        for c in range(parts):
            by_row_chunks(c * tp, tp, norm_out(acc_ref))
            y_copy(i, c).start()

        @pl.when(i == pl.num_programs(0) - 1)
        def _():
            for c in range(parts):
                y_copy(i, c).wait()

        if rider:
            @pl.when(i == 0)
            def _():
                by_row_chunks(0, acc_s.shape[0], norm_out(acc_s))
                cp = pltpu.make_async_copy(acc_s, ys_hbm, s_sem.at[1])
                cp.start()
                cp.wait()


def _cache_roll_step(step, roll, caches, new_ref, rolled, slots, in_sem, out_sem, new_sem):
    n_chunks, per_seq, n_heads = roll
    rows = slots.shape[2]
    n_new = new_ref.shape[1]

    def chunk_copies(c):
        seq, j, slot = c // per_seq, c % per_seq, c % 2
        pairs = []
        for i in range(2):
            buf = slots.at[slot, i]
            pairs.append((
                pltpu.make_async_copy(caches[i].at[seq, pl.ds(j * rows + n_new, rows)], buf, in_sem.at[slot, i]),
                pltpu.make_async_copy(buf, rolled[i].at[seq, pl.ds(j * rows, rows)], out_sem.at[slot, i])))
        return pairs

    def new_copies(c):
        seq = c // per_seq
        return [pltpu.make_async_copy(new_ref.at[seq, :, pl.ds((i + 1) * n_heads, n_heads)],
                                      rolled[i].at[seq, pl.ds(per_seq * rows, n_new)], new_sem.at[i])
                for i in range(2)]

    @pl.when((step >= 2) & (step < n_chunks + 2))
    def _():
        for _, out in chunk_copies(step - 2):
            out.wait()

    @pl.when(step < n_chunks)
    def _():
        for inp, _ in chunk_copies(step):
            inp.start()

    @pl.when((step >= 1) & (step < n_chunks + 1))
    def _():
        for inp, out in chunk_copies(step - 1):
            inp.wait()
            out.start()

    @pl.when((step >= 1) & (step < n_chunks + 1) & ((step - 1) % per_seq == 0))
    def _():
        for cp in new_copies(step - 1):
            cp.wait()

    @pl.when((step < n_chunks) & (step % per_seq == 0))
    def _():
        for cp in new_copies(step):
            cp.start()


def _ffn(x, g2, w_gate, w_up, w_down, gf, rider=None, caches=None):
    m, d = x.shape
    dff = w_gate.shape[1]
    tm = _tile(m, 1024, 16)
    tf = _tile(dff, 256, LANES)
    nf = dff // tf
    grid = (m // tm, nf)
    vec = pl.BlockSpec((1, d), lambda i, f: (0, 0))
    any_spec = pl.BlockSpec(memory_space=pl.ANY)
    in_specs = [any_spec, vec,
                pl.BlockSpec((d, tf), lambda i, f: (0, f)),
                pl.BlockSpec((d, tf), lambda i, f: (0, f)),
                pl.BlockSpec((tf, d), lambda i, f: (f, 0)), vec]
    out_specs = [any_spec]
    out_shape = [jax.ShapeDtypeStruct((m, d), F32)]
    args = [x, g2, w_gate, w_up, w_down, gf]
    scratch = [pltpu.VMEM((tm, d), F32), pltpu.VMEM((tm, d), BF16),
               pltpu.SemaphoreType.DMA((FFN_TRANSFER_PARTS,)), pltpu.SemaphoreType.DMA((FFN_TRANSFER_PARTS,))]
    if rider is not None:
        in_specs.append(any_spec)
        out_specs.append(any_spec)
        out_shape.append(jax.ShapeDtypeStruct(rider.shape, F32))
        args.append(rider)
        scratch += [pltpu.VMEM(rider.shape, F32), pltpu.VMEM(rider.shape, BF16), pltpu.SemaphoreType.DMA((2,))]
    roll = None
    if caches is not None:
        cache_k, cache_v, qkv_new = caches
        b, buf, n_heads, _ = cache_k.shape
        n_new = qkv_new.shape[1]
        assert buf == MAX_WINDOW, "the rolled cache keeps exactly the window"
        old = buf - n_new
        steps = grid[0] * grid[1]
        per_seq = max(p for p in range(2, old + 1) if old % p == 0 and b * p + 2 <= steps)
        n_chunks = b * per_seq
        in_specs += [any_spec] * 3
        out_specs += [any_spec] * 2
        out_shape += [jax.ShapeDtypeStruct((b, buf, n_heads, HEAD_DIM), F32)] * 2
        args += [cache_k, cache_v, qkv_new]
        scratch += [pltpu.VMEM((2, 2, old // per_seq, n_heads, HEAD_DIM), F32),
                    pltpu.SemaphoreType.DMA((2, 2)), pltpu.SemaphoreType.DMA((2, 2)),
                    pltpu.SemaphoreType.DMA((2,))]
        roll = (n_chunks, per_seq, n_heads)
    return pl.pallas_call(
        functools.partial(_ffn_kernel, roll=roll, rider=rider is not None),
        grid=grid,
        in_specs=in_specs,
        out_specs=out_specs,
        out_shape=out_shape,
        scratch_shapes=scratch,
        compiler_params=_params("arbitrary", "arbitrary"),
        name="ffn",
    )(*args)


def kernel(x_prompt, x_sample, cache_k, cache_v, state_conv, norm1_g, w_in, conv_w, conv_b, conv_ln_g,
           conv_ln_b, w_out, norm2_g, w_gate, w_up, w_down, final_g):
    depth = w_in.shape[0]
    bp, s, d = x_prompt.shape
    bs, t_new, _ = x_sample.shape
    assert depth == 1 and bp == 1, "one layer, one prompt sequence"
    attn_w = d // 2
    conv_c = d - attn_w
    n_heads = attn_w // HEAD_DIM
    buf = cache_k.shape[2]

    row = lambda v: v.reshape(1, -1)
    l = 0
    w_in_b, w_out_b = w_in[l], w_out[l]
    g1, g2, gf = row(norm1_g[l]), row(norm2_g[l]), row(final_g)
    cb, lg, lb = row(conv_b[l]), row(conv_ln_g[l]), row(conv_ln_b[l])

    ms = bs * t_new
    xs = x_sample.reshape(ms, d)
    pos_s = PAST_LEN + jnp.tile(jnp.arange(t_new, dtype=jnp.int32), bs)
    qkv_s, u_s = _project_in(xs, g1, w_in_b, pos_s, attn_w, conv_c)
    qkv_s = qkv_s.reshape(bs, t_new, 3 * n_heads, HEAD_DIM)
    attn_s = _attn_sample(qkv_s, cache_k[l], cache_v[l])
    conv_s, nc_s = _conv_sample(state_conv[l], u_s.reshape(bs, t_new, conv_c), conv_w[l], cb, lg, lb)
    x1_s = _outproj(xs, attn_s.reshape(ms, attn_w), conv_s.reshape(ms, conv_c), w_out_b)

    xp = x_prompt.reshape(s, d)
    qkv_p, u_p = _project_in(xp, g1, w_in_b, jnp.arange(s, dtype=jnp.int32), attn_w, conv_c)
    attn_p, (w_gate_b, w_up_b, w_down_b) = _attn_prompt(qkv_p, n_heads, (w_gate[l], w_up[l], w_down[l]))
    conv_p = _conv_prompt(u_p, conv_w[l], cb, lg, lb)
    x1_p = _outproj(xp, attn_p, conv_p, w_out_b)
    y_p, y_s, nk_s, nv_s = _ffn(x1_p, g2, w_gate_b, w_up_b, w_down_b, gf, rider=x1_s,
                                caches=(cache_k[l], cache_v[l], qkv_s))
    keep_p = min(MAX_WINDOW, s)
    nk_p = qkv_p[s - keep_p:, attn_w:2 * attn_w].reshape(1, 1, keep_p, n_heads, HEAD_DIM)
    nv_p = qkv_p[s - keep_p:, 2 * attn_w:].reshape(1, 1, keep_p, n_heads, HEAD_DIM)
    n_state = CONV_K - 1
    assert s >= n_state
    nc_p = u_p[s - n_state:].reshape(1, 1, n_state, conv_c)

    return (y_p.reshape(1, s, d), y_s.reshape(bs, t_new, d), nk_p, nv_p, nc_p,
            nk_s[None], nv_s[None], nc_s.reshape(1, bs, n_state, conv_c))
```

```python
import functools
import math

import jax
import jax.numpy as jnp
from jax import lax
from jax.experimental import pallas as pl
from jax.experimental.pallas import tpu as pltpu

F32 = jnp.float32
BF16 = jnp.bfloat16

HEAD_DIM = 128
ROPE_DIM = HEAD_DIM // 4
ROPE_THETA = 500000.0
CONV_K = 31
RMS_EPS = 1e-6
LN_EPS = 1e-5
PAST_LEN = 8192
DILATIONS = (1, 4, 16)
SPAN = 128
MAX_WINDOW = 2048
ATTN_TILE = MAX_WINDOW
ATTN_UNROLL = 16
CONV_HALO = 32
FFN_DOWN_CHUNK = 1024
NORM_ROWS = 64
FFN_TRANSFER_PARTS = 4
WEIGHT_CAST_ROWS = 512

LANES = 128
VMEM_LIMIT_BYTES = 56 * 1024 * 1024


def _params(*sem):
    return pltpu.CompilerParams(dimension_semantics=sem, vmem_limit_bytes=VMEM_LIMIT_BYTES)


def _tile(n, pref, mult=8):
    if n <= pref:
        return n
    for t in range(pref - pref % mult, 0, -mult):
        if n % t == 0:
            return t
    raise ValueError(f"no tile for {n}")


def _rms_to_bf16(x, g):
    ms = jnp.mean(x * x, axis=-1, keepdims=True)
    return (x * lax.rsqrt(ms + RMS_EPS) * g).astype(BF16)


def _dot(a, b):
    return jnp.dot(a, b, preferred_element_type=F32)


def _dot_nt(a, b):
    return lax.dot_general(a, b, (((1,), (1,)), ((), ())), preferred_element_type=F32)


def _norm_kernel(x_ref, g_ref, h_ref):
    h_ref[...] = _rms_to_bf16(x_ref[...], g_ref[...])


def _cast_weight(w_ref, w_bf):
    k = w_ref.shape[0]
    rc = _tile(k, WEIGHT_CAST_ROWS)
    for r0 in range(0, k, rc):
        w_bf[r0:r0 + rc, :] = w_ref[r0:r0 + rc, :].astype(BF16)


def _row_parts(tm):
    half = tm // 2
    return ((0, tm),) if half % 16 else ((0, half), (half, tm))


def _qkv_kernel(h_ref, w_ref, tab_ref, o_ref, w_bf):
    @pl.when(pl.program_id(1) == 0)
    def _():
        _cast_weight(w_ref, w_bf)

    for r0, r1 in _row_parts(h_ref.shape[0]):
        z = _dot(h_ref[r0:r1, :], w_bf[...])
        c, s1, s2 = tab_ref[0, 0, r0:r1, :], tab_ref[0, 1, r0:r1, :], tab_ref[0, 2, r0:r1, :]
        for j in range(z.shape[1] // HEAD_DIM):
            zj = z[:, j * HEAD_DIM:(j + 1) * HEAD_DIM]
            lo = pltpu.roll(zj, ROPE_DIM // 2, 1)
            hi = pltpu.roll(zj, HEAD_DIM - ROPE_DIM // 2, 1)
            o_ref[r0:r1, j * HEAD_DIM:(j + 1) * HEAD_DIM] = zj * c + lo * s1 + hi * s2


def _glu_kernel(h_ref, wa_ref, wg_ref, u_ref, wa_bf, wg_bf):
    @pl.when(pl.program_id(1) == 0)
    def _():
        _cast_weight(wa_ref, wa_bf)
        _cast_weight(wg_ref, wg_bf)

    for r0, r1 in _row_parts(h_ref.shape[0]):
        h = h_ref[r0:r1, :]
        u_ref[r0:r1, :] = _dot(h, wa_bf[...]) * jax.nn.sigmoid(_dot(h, wg_bf[...]))


def _rope_tables(pos):
    half = ROPE_DIM // 2
    inv = ROPE_THETA ** (-jnp.arange(half, dtype=F32) / half)
    ang = pos.astype(F32)[:, None] * inv[None, :]
    cos, sin = jnp.cos(ang), jnp.sin(ang)
    m = pos.shape[0]
    one = jnp.ones((m, HEAD_DIM - ROPE_DIM), F32)
    zero = jnp.zeros((m, HEAD_DIM - ROPE_DIM), F32)
    zh = jnp.zeros((m, half), F32)
    c = jnp.concatenate([cos, cos, one], axis=1)
    s1 = jnp.concatenate([zh, sin, zero], axis=1)
    s2 = jnp.concatenate([-sin, zh, zero], axis=1)
    rot = jnp.stack([c, s1, s2])
    ident = jnp.stack([jnp.ones_like(c), jnp.zeros_like(c), jnp.zeros_like(c)])
    return jnp.stack([rot, ident])


def _project_in(x, g1, w_in, pos, attn_w, conv_c):
    m, d = x.shape
    tr = _tile(m, 256)
    h = pl.pallas_call(
        _norm_kernel,
        grid=(m // tr,),
        in_specs=[pl.BlockSpec((tr, d), lambda i: (i, 0)), pl.BlockSpec((1, d), lambda i: (0, 0))],
        out_specs=pl.BlockSpec((tr, d), lambda i: (i, 0)),
        out_shape=jax.ShapeDtypeStruct((m, d), BF16),
        compiler_params=_params("parallel"),
        name="norm1",
    )(x, g1)

    tm = _tile(m, 1024, 16)
    tn = _tile(attn_w, 1024, LANES)
    once = pl.Buffered(1) if m // tm > 1 else None
    n_rope = 2 * attn_w // tn
    qkv = pl.pallas_call(
        _qkv_kernel,
        grid=(3 * attn_w // tn, m // tm),
        in_specs=[pl.BlockSpec((tm, d), lambda n, i: (i, 0)),
                  pl.BlockSpec((d, tn), lambda n, i: (0, n), pipeline_mode=once),
                  pl.BlockSpec((1, 3, tm, HEAD_DIM), lambda n, i: (jnp.where(n < n_rope, 0, 1), 0, i, 0))],
        out_specs=pl.BlockSpec((tm, tn), lambda n, i: (i, n)),
        out_shape=jax.ShapeDtypeStruct((m, 3 * attn_w), F32),
        scratch_shapes=[pltpu.VMEM((d, tn), BF16)],
        compiler_params=_params("arbitrary", "arbitrary"),
        name="qkv_proj",
    )(h, w_in, _rope_tables(pos))

    tc = _tile(conv_c, 512, LANES)
    a0 = 3 * attn_w // tc
    g0 = (3 * attn_w + conv_c) // tc
    u = pl.pallas_call(
        _glu_kernel,
        grid=(conv_c // tc, m // tm),
        in_specs=[pl.BlockSpec((tm, d), lambda n, i: (i, 0)),
                  pl.BlockSpec((d, tc), lambda n, i: (0, a0 + n), pipeline_mode=once),
                  pl.BlockSpec((d, tc), lambda n, i: (0, g0 + n), pipeline_mode=once)],
        out_specs=pl.BlockSpec((tm, tc), lambda n, i: (i, n)),
        out_shape=jax.ShapeDtypeStruct((m, conv_c), F32),
        scratch_shapes=[pltpu.VMEM((d, tc), BF16)] * 2,
        compiler_params=_params("arbitrary", "arbitrary"),
        name="glu_proj",
    )(h, w_in, w_in)
    return qkv, u


def _rows(start, size, stride):
    return pl.ds(start, size) if stride == 1 else pl.ds(start, size, stride=stride)


def _attn_prompt_kernel(q_ref, kp_ref, kc_ref, vp_ref, vc_ref, *rest):
    n_w = (len(rest) - 4) // 2
    w_in, o_ref, w_out = rest[:n_w], rest[n_w], rest[n_w + 1:2 * n_w + 1]
    num_scr, m_scr, l_scr = rest[2 * n_w + 1:]
    for src, dst in zip(w_in, w_out):
        rc = _tile(src.shape[0], 16, 16)
        for r0 in range(0, src.shape[0], rc):
            dst[r0:r0 + rc, :] = src[r0:r0 + rc, :].astype(BF16)

    tile = pl.program_id(1)
    t = q_ref.shape[0]
    scale = HEAD_DIM ** -0.5
    qi = lax.broadcasted_iota(jnp.int32, (SPAN, 2 * SPAN), 0)
    kj = lax.broadcasted_iota(jnp.int32, (SPAN, 2 * SPAN), 1)
    band = (kj >= qi) & (kj <= qi + SPAN)
    first_key = jnp.where(tile > 0, 0, SPAN)
    band_first = band & (kj >= first_key)

    def attend(gi, d, q_start, q, k, v, mask):
        s = _dot_nt(q.astype(BF16), k.astype(BF16)) * scale
        s = jnp.where(mask, s, -jnp.inf)
        m = jnp.max(s, axis=-1, keepdims=True)
        p = jnp.exp(s - m)
        l = jnp.sum(p, axis=-1, keepdims=True)
        num = _dot(p.astype(BF16), v.astype(BF16))
        rows = _rows(q_start, SPAN, d)
        num_scr[gi, rows, :] = num
        m_scr[gi, rows, :] = jnp.broadcast_to(m, (SPAN, HEAD_DIM))
        l_scr[gi, rows, :] = jnp.broadcast_to(l, (SPAN, HEAD_DIM))

    def first_block(gi, d, r):
        q = q_ref[_rows(r, SPAN, d), :]
        k = jnp.concatenate([kp_ref[_rows(t - d * SPAN + r, SPAN, d), :],
                             kc_ref[_rows(r, SPAN, d), :]], axis=0)
        v = jnp.concatenate([vp_ref[_rows(t - d * SPAN + r, SPAN, d), :],
                             vc_ref[_rows(r, SPAN, d), :]], axis=0)
        attend(gi, d, r, q, k, v, band_first)

    def later_block(gi, d, r, b):
        q_start = r + d * SPAN * b
        if d == 1:
            q_start = pl.multiple_of(q_start, SPAN)
        k_rows = _rows(q_start - d * SPAN, 2 * SPAN, d)
        attend(gi, d, q_start, q_ref[_rows(q_start, SPAN, d), :], kc_ref[k_rows, :], vc_ref[k_rows, :], band)

    u = ATTN_UNROLL
    for gi, d in enumerate(DILATIONS):
        n_blocks = t // (d * SPAN)
        if n_blocks >= u:
            assert n_blocks % u == 0

            def class_body(r, carry, gi=gi, d=d, n_blocks=n_blocks):
                first_block(gi, d, r)
                for b in range(1, u):
                    later_block(gi, d, r, b)

                def rest(i, c):
                    for j in range(u):
                        later_block(gi, d, r, i * u + j)
                    return c

                if n_blocks > u:
                    lax.fori_loop(1, n_blocks // u, rest, 0)
                return carry

            if d == 1:
                class_body(0, 0)
            else:
                lax.fori_loop(0, d, class_body, 0)
        else:
            per_iter = u // n_blocks
            assert u % n_blocks == 0 and d % per_iter == 0

            def class_group(i, carry, gi=gi, d=d, n_blocks=n_blocks, per_iter=per_iter):
                for j in range(per_iter):
                    r = i * per_iter + j
                    first_block(gi, d, r)
                    for b in range(1, n_blocks):
                        later_block(gi, d, r, b)
                return carry

            lax.fori_loop(0, d // per_iter, class_group, 0)

    def combine(c, carry):
        rows = pl.ds(pl.multiple_of(c * SPAN, SPAN), SPAN)
        ms = [m_scr[g, rows, :] for g in range(len(DILATIONS))]
        m_all = functools.reduce(jnp.maximum, ms)
        num = jnp.zeros((SPAN, HEAD_DIM), F32)
        den = jnp.zeros((SPAN, HEAD_DIM), F32)
        for g in range(len(DILATIONS)):
            sc = jnp.exp(ms[g] - m_all)
            num = num + num_scr[g, rows, :] * sc
            den = den + l_scr[g, rows, :] * sc
        o_ref[rows, :] = (num / den).astype(o_ref.dtype)
        return carry

    lax.fori_loop(0, t // SPAN, combine, 0)


def _attn_prompt(qkv, n_heads, weights=()):
    s = qkv.shape[0]
    t = ATTN_TILE
    assert s % t == 0, "prompt length must be a multiple of the dilation tile"
    n_tiles = s // t
    cur = lambda off: pl.BlockSpec((t, HEAD_DIM), lambda h, i: (i, off + h))
    prev = lambda off: pl.BlockSpec((t, HEAD_DIM), lambda h, i: (jnp.maximum(i - 1, 0), off + h))
    w_specs, w_shapes = [], []
    for w in weights:
        rows, cols = w.shape
        rb = next(r for r in range(16, rows + 1, 16) if rows % r == 0 and rows // r <= n_heads * n_tiles)
        spec = pl.BlockSpec((rb, cols), lambda h, i, nb=rows // rb: (jnp.minimum(h * n_tiles + i, nb - 1), 0))
        w_specs.append(spec)
        w_shapes.append(jax.ShapeDtypeStruct(w.shape, BF16))
    out = pl.pallas_call(
        _attn_prompt_kernel,
        grid=(n_heads, n_tiles),
        in_specs=[cur(0), prev(n_heads), cur(n_heads), prev(2 * n_heads), cur(2 * n_heads)] + w_specs,
        out_specs=[pl.BlockSpec((t, HEAD_DIM), lambda h, i: (i, h))] + w_specs,
        out_shape=[jax.ShapeDtypeStruct((s, n_heads * HEAD_DIM), BF16)] + w_shapes,
        scratch_shapes=[pltpu.VMEM((len(DILATIONS), t, HEAD_DIM), F32)] * 3,
        compiler_params=_params("arbitrary", "arbitrary"),
        name="attn_prompt",
    )(qkv, qkv, qkv, qkv, qkv, *weights)
    return out[0], tuple(out[1:])


def _attn_sample_kernel(qkv_ref, kt_ref, vt_ref, kf_ref, vf_ref, o_ref, kn_scr, vn_scr, *, n_heads):
    n_new = qkv_ref.shape[1]
    tail = kt_ref.shape[1]
    d_far = DILATIONS[-1]
    kn_scr[0:tail] = kt_ref[0]
    vn_scr[0:tail] = vt_ref[0]
    kn_scr[tail:tail + n_new] = qkv_ref[0, :, n_heads:2 * n_heads, :]
    vn_scr[tail:tail + n_new] = qkv_ref[0, :, 2 * n_heads:3 * n_heads, :]
    n_near_far = tail // d_far + 1

    for i in range(n_new):
        q = qkv_ref[0, i, 0:n_heads, :] * (HEAD_DIM ** -0.5)
        lists = []
        for d in DILATIONS[:-1]:
            rows = _rows(tail + i - d * SPAN, SPAN + 1, d)
            lists.append((kn_scr[rows], vn_scr[rows]))
        rows = _rows(tail + i - d_far * (n_near_far - 1), n_near_far, d_far)
        lists.append((kn_scr[rows], vn_scr[rows]))
        lists.append((kf_ref[0, :, i], vf_ref[0, :, i]))
        scores = [jnp.sum(k * q[None], axis=-1, keepdims=True) for k, _ in lists]
        m = functools.reduce(jnp.maximum, [jnp.max(s, axis=0) for s in scores])
        den = jnp.zeros((n_heads, 1), F32)
        num = jnp.zeros((n_heads, HEAD_DIM), F32)
        for s, (_, v) in zip(scores, lists):
            p = jnp.exp(s - m[None])
            den = den + jnp.sum(p, axis=0)
            num = num + jnp.sum(p * v, axis=0)
        o_ref[0, i] = num / den


def _attn_sample(qkv, cache_k, cache_v):
    b, n_new, _, _ = qkv.shape
    buf, n_heads = cache_k.shape[1], cache_k.shape[2]
    d_mid, d_far = DILATIONS[1], DILATIONS[2]
    tail = d_mid * SPAN
    assert buf == d_far * SPAN and n_new <= d_mid and tail % d_far == 0
    n_far = (buf - tail) // d_far
    far_shape = (b, buf // d_far, d_far, n_heads, HEAD_DIM)
    near = pl.BlockSpec((1, tail, n_heads, HEAD_DIM), lambda i: (i, buf // tail - 1, 0, 0))
    far = pl.BlockSpec((1, n_far, n_new, n_heads, HEAD_DIM), lambda i: (i, 0, 0, 0, 0))
    new = pl.BlockSpec((1, n_new, 3 * n_heads, HEAD_DIM), lambda i: (i, 0, 0, 0))
    ctx = pltpu.VMEM((tail + 8, n_heads, HEAD_DIM), F32)
    return pl.pallas_call(
        functools.partial(_attn_sample_kernel, n_heads=n_heads),
        grid=(b,),
        in_specs=[new, near, near, far, far],
        out_specs=pl.BlockSpec((1, n_new, n_heads, HEAD_DIM), lambda i: (i, 0, 0, 0)),
        out_shape=jax.ShapeDtypeStruct((b, n_new, n_heads, HEAD_DIM), F32),
        scratch_shapes=[ctx, ctx],
        compiler_params=_params("parallel"),
        name="attn_sample",
    )(qkv, cache_k, cache_v, cache_k.reshape(far_shape), cache_v.reshape(far_shape))


def _ln_swish(y, g, b):
    mu = jnp.mean(y, axis=-1, keepdims=True)
    yc = y - mu
    var = jnp.mean(yc * yc, axis=-1, keepdims=True)
    z = yc * lax.rsqrt(var + LN_EPS) * g + b
    return z * jax.nn.sigmoid(z)


def _conv_prompt_kernel(up_ref, u_ref, w_ref, b_ref, lg_ref, lb_ref, o_ref, ctx_scr, y_scr):
    tm, c = u_ref.shape
    n_slabs = c // LANES
    slab = lambda cs: slice(cs * LANES, (cs + 1) * LANES)

    @pl.when(pl.program_id(0) > 0)
    def _():
        for cs in range(n_slabs):
            ctx_scr[cs, 0:CONV_HALO, :] = up_ref[:, slab(cs)]

    @pl.when(pl.program_id(0) == 0)
    def _():
        for cs in range(n_slabs):
            ctx_scr[cs, 0:CONV_HALO, :] = jnp.zeros((CONV_HALO, LANES), F32)

    for cs in range(n_slabs):
        ctx_scr[cs, CONV_HALO:CONV_HALO + tm, :] = u_ref[:, slab(cs)]

    off = CONV_HALO - (CONV_K - 1)
    rc = 32

    for cs in range(n_slabs):
        taps = [jnp.broadcast_to(w_ref[j:j + 1, slab(cs)], (rc, LANES)) for j in range(CONV_K)]
        bias = jnp.broadcast_to(b_ref[:, slab(cs)], (rc, LANES))

        def chunk(i, carry, cs=cs, taps=taps, bias=bias):
            r0 = i * (2 * rc)
            for phase in range(2):
                acc = bias
                for j in range(CONV_K):
                    acc = acc + ctx_scr[cs, pl.ds(r0 + phase + off + j, rc, stride=2), :] * taps[j]
                y_scr[cs, pl.ds(r0 + phase, rc, stride=2), :] = acc
            return carry

        lax.fori_loop(0, tm // (2 * rc), chunk, 0)

    ln_rows = 16
    ln_unroll = 4

    def ln_rows_at(row0):
        rows = pl.ds(pl.multiple_of(row0, ln_rows), ln_rows)
        ys = [y_scr[cs, rows, :] for cs in range(n_slabs)]
        mu = jnp.sum(functools.reduce(jnp.add, ys), axis=-1, keepdims=True) * (1.0 / c)
        ycs = [y - mu for y in ys]
        var = jnp.sum(functools.reduce(jnp.add, [yc * yc for yc in ycs]), axis=-1, keepdims=True) * (1.0 / c)
        inv = lax.rsqrt(var + LN_EPS)
        for cs in range(n_slabs):
            z = ycs[cs] * inv * lg_ref[:, slab(cs)] + lb_ref[:, slab(cs)]
            o_ref[rows, slab(cs)] = (z * jax.nn.sigmoid(z)).astype(o_ref.dtype)

    def ln_chunk(i, carry):
        for j in range(ln_unroll):
            ln_rows_at((i * ln_unroll + j) * ln_rows)
        return carry

    lax.fori_loop(0, tm // (ln_rows * ln_unroll), ln_chunk, 0)


def _conv_prompt(u, conv_w, conv_b, ln_g, ln_b):
    s, c = u.shape
    tm = _tile(s, 256, CONV_HALO)
    vec = pl.BlockSpec((1, c), lambda i: (0, 0))
    return pl.pallas_call(
        _conv_prompt_kernel,
        grid=(s // tm,),
        in_specs=[pl.BlockSpec((CONV_HALO, c), lambda i: (jnp.maximum(i * (tm // CONV_HALO) - 1, 0), 0)),
                  pl.BlockSpec((tm, c), lambda i: (i, 0)),
                  pl.BlockSpec((CONV_K, c), lambda i: (0, 0)), vec, vec, vec],
        out_specs=pl.BlockSpec((tm, c), lambda i: (i, 0)),
        out_shape=jax.ShapeDtypeStruct((s, c), BF16),
        scratch_shapes=[pltpu.VMEM((c // LANES, CONV_HALO + tm, LANES), F32),
                        pltpu.VMEM((c // LANES, tm, LANES), F32)],
        compiler_params=_params("parallel"),
        name="conv_prompt",
    )(u, u, conv_w, conv_b, ln_g, ln_b)


def _conv_sample_kernel(st_ref, u_ref, w_ref, b_ref, lg_ref, lb_ref, o_ref, ns_ref, ctx_scr):
    n_state, n_new = st_ref.shape[1], u_ref.shape[1]
    ctx_scr[0:n_state, :] = st_ref[0]
    ctx_scr[n_state:n_state + n_new, :] = u_ref[0]
    acc = jnp.zeros(u_ref.shape[1:], F32)
    for j in range(CONV_K):
        acc = acc + ctx_scr[j:j + n_new, :] * w_ref[j:j + 1, :]
    o_ref[0] = _ln_swish(acc + b_ref[...], lg_ref[...], lb_ref[...])
    ns_ref[0] = ctx_scr[n_new:n_new + n_state, :]


def _conv_sample(state, u, conv_w, conv_b, ln_g, ln_b):
    b, n_state, c = state.shape
    n_new = u.shape[1]
    assert n_state == CONV_K - 1
    vec = pl.BlockSpec((1, c), lambda i: (0, 0))
    return pl.pallas_call(
        _conv_sample_kernel,
        grid=(b,),
        in_specs=[pl.BlockSpec((1, n_state, c), lambda i: (i, 0, 0)),
                  pl.BlockSpec((1, n_new, c), lambda i: (i, 0, 0)),
                  pl.BlockSpec((CONV_K, c), lambda i: (0, 0)), vec, vec, vec],
        out_specs=[pl.BlockSpec((1, n_new, c), lambda i: (i, 0, 0)),
                   pl.BlockSpec((1, n_state, c), lambda i: (i, 0, 0))],
        out_shape=[jax.ShapeDtypeStruct((b, n_new, c), F32),
                   jax.ShapeDtypeStruct((b, n_state, c), F32)],
        scratch_shapes=[pltpu.VMEM((n_state + n_new, c), F32)],
        compiler_params=_params("parallel"),
        name="conv_sample",
    )(state, u, conv_w, conv_b, ln_g, ln_b)


def _outproj_kernel(x_ref, a_ref, c_ref, wa_ref, wc_ref, o_ref, wa_bf, wc_bf):
    @pl.when(pl.program_id(1) == 0)
    def _():
        _cast_weight(wa_ref, wa_bf)
        _cast_weight(wc_ref, wc_bf)

    for r0, r1 in _row_parts(x_ref.shape[0]):
        acc = _dot(a_ref[r0:r1, :].astype(BF16), wa_bf[...]) + _dot(c_ref[r0:r1, :].astype(BF16), wc_bf[...])
        o_ref[r0:r1, :] = x_ref[r0:r1, :] + acc


def _outproj(x, attn, conv, w_out):
    m, d = x.shape
    a, c = attn.shape[1], conv.shape[1]
    assert a == c
    tm = _tile(m, 1024, 16)
    tn = _tile(d, 512, LANES)
    return pl.pallas_call(
        _outproj_kernel,
        grid=(d // tn, m // tm),
        in_specs=[pl.BlockSpec((tm, tn), lambda n, i: (i, n)),
                  pl.BlockSpec((tm, a), lambda n, i: (i, 0)),
                  pl.BlockSpec((tm, c), lambda n, i: (i, 0)),
                  pl.BlockSpec((a, tn), lambda n, i: (0, n)),
                  pl.BlockSpec((c, tn), lambda n, i: (1, n))],
        out_specs=pl.BlockSpec((tm, tn), lambda n, i: (i, n)),
        out_shape=jax.ShapeDtypeStruct((m, d), F32),
        scratch_shapes=[pltpu.VMEM((a, tn), BF16), pltpu.VMEM((c, tn), BF16)],
        compiler_params=_params("arbitrary", "arbitrary"),
        name="out_proj",
    )(x, attn, conv, w_out, w_out)


def _ffn_kernel(*refs, roll, rider):
    refs = list(refs)
    take = lambda n: [refs.pop(0) for _ in range(n)]
    x_hbm, g2_ref, wg_ref, wu_ref, wd_ref, gf_ref = take(6)
    xs_hbm, = take(1) if rider else (None,)
    caches = take(3) if roll else None
    y_hbm, = take(1)
    ys_hbm, = take(1) if rider else (None,)
    rolled = take(2) if roll else None
    acc_ref, h_ref, x_sem, y_sem = take(4)
    acc_s, h_s, s_sem = take(3) if rider else (None, None, None)
    i, f = pl.program_id(0), pl.program_id(1)
    last_f = pl.num_programs(1) - 1
    if roll:
        slots, in_sem, out_sem, new_sem = take(4)
        _cache_roll_step(i * pl.num_programs(1) + f, roll, caches[:2], caches[2], rolled,
                         slots, in_sem, out_sem, new_sem)
    tm, d = acc_ref.shape
    parts = FFN_TRANSFER_PARTS if tm % (16 * FFN_TRANSFER_PARTS) == 0 else 1
    tp = tm // parts

    def x_copy(t, c):
        return pltpu.make_async_copy(x_hbm.at[pl.ds(pl.multiple_of(t * tm + c * tp, tp), tp)],
                                     acc_ref.at[pl.ds(c * tp, tp)], x_sem.at[c])

    def y_copy(t, c):
        return pltpu.make_async_copy(acc_ref.at[pl.ds(c * tp, tp)],
                                     y_hbm.at[pl.ds(pl.multiple_of(t * tm + c * tp, tp), tp)], y_sem.at[c])

    def by_row_chunks(start, size, body):
        rc = _tile(size, NORM_ROWS)

        def step(c, carry):
            body(pl.ds(pl.multiple_of(start + c * rc, rc), rc))
            return carry
        lax.fori_loop(0, size // rc, step, 0)

    def norm_in(acc, h):
        def body(rows):
            h[rows, :] = _rms_to_bf16(acc[rows, :], g2_ref[...])
        return body

    def norm_out(acc):
        def body(rows):
            y = acc[rows, :]
            ms = jnp.mean(y * y, axis=-1, keepdims=True)
            acc[rows, :] = y * lax.rsqrt(ms + RMS_EPS) * gf_ref[...]
        return body

    def swiglu_into(acc, h_rows):
        h = h_rows[...]
        gate = _dot(h, wg_ref[...])
        act = (gate * jax.nn.sigmoid(gate) * _dot(h, wu_ref[...])).astype(BF16)
        dc = _tile(d, FFN_DOWN_CHUNK, LANES)
        for c0 in range(0, d, dc):
            acc[:, c0:c0 + dc] += _dot(act, wd_ref[:, c0:c0 + dc])

    @pl.when(f == 0)
    def _():
        for c in range(parts):
            @pl.when(i > 0)
            def _(c=c):
                y_copy(i - 1, c).wait()
            x_copy(i, c).start()
        for c in range(parts):
            x_copy(i, c).wait()
            by_row_chunks(c * tp, tp, norm_in(acc_ref, h_ref))
        if rider:
            @pl.when(i == 0)
            def _():
                cp = pltpu.make_async_copy(xs_hbm, acc_s, s_sem.at[0])
                cp.start()
                cp.wait()
                by_row_chunks(0, acc_s.shape[0], norm_in(acc_s, h_s))

    swiglu_into(acc_ref, h_ref)
    if rider:
        @pl.when(i == 0)
        def _():
            swiglu_into(acc_s, h_s)

    @pl.when(f == last_f)
    def _():
        for c in range(parts):
            by_row_chunks(c * tp, tp, norm_out(acc_ref))
            y_copy(i, c).start()

        @pl.when(i == pl.num_programs(0) - 1)
        def _():
            for c in range(parts):
                y_copy(i, c).wait()

        if rider:
            @pl.when(i == 0)
            def _():
                by_row_chunks(0, acc_s.shape[0], norm_out(acc_s))
                cp = pltpu.make_async_copy(acc_s, ys_hbm, s_sem.at[1])
                cp.start()
                cp.wait()


def _cache_roll_step(step, roll, caches, new_ref, rolled, slots, in_sem, out_sem, new_sem):
    n_chunks, per_seq, n_heads = roll
    rows = slots.shape[2]
    n_new = new_ref.shape[1]

    def chunk_copies(c):
        seq, j, slot = c // per_seq, c % per_seq, c % 2
        pairs = []
        for i in range(2):
            buf = slots.at[slot, i]
            pairs.append((
                pltpu.make_async_copy(caches[i].at[seq, pl.ds(j * rows + n_new, rows)], buf, in_sem.at[slot, i]),
                pltpu.make_async_copy(buf, rolled[i].at[seq, pl.ds(j * rows, rows)], out_sem.at[slot, i])))
        return pairs

    def new_copies(c):
        seq = c // per_seq
        return [pltpu.make_async_copy(new_ref.at[seq, :, pl.ds((i + 1) * n_heads, n_heads)],
                                      rolled[i].at[seq, pl.ds(per_seq * rows, n_new)], new_sem.at[i])
                for i in range(2)]

    @pl.when((step >= 2) & (step < n_chunks + 2))
    def _():
        for _, out in chunk_copies(step - 2):
            out.wait()

    @pl.when(step < n_chunks)
    def _():
        for inp, _ in chunk_copies(step):
            inp.start()

    @pl.when((step >= 1) & (step < n_chunks + 1))
    def _():
        for inp, out in chunk_copies(step - 1):
            inp.wait()
            out.start()

    @pl.when((step >= 1) & (step < n_chunks + 1) & ((step - 1) % per_seq == 0))
    def _():
        for cp in new_copies(step - 1):
            cp.wait()

    @pl.when((step < n_chunks) & (step % per_seq == 0))
    def _():
        for cp in new_copies(step):
            cp.start()


def _ffn(x, g2, w_gate, w_up, w_down, gf, rider=None, caches=None):
    m, d = x.shape
    dff = w_gate.shape[1]
    tm = _tile(m, 1024, 16)
    tf = _tile(dff, 256, LANES)
    nf = dff // tf
    grid = (m // tm, nf)
    vec = pl.BlockSpec((1, d), lambda i, f: (0, 0))
    any_spec = pl.BlockSpec(memory_space=pl.ANY)
    in_specs = [any_spec, vec,
                pl.BlockSpec((d, tf), lambda i, f: (0, f)),
                pl.BlockSpec((d, tf), lambda i, f: (0, f)),
                pl.BlockSpec((tf, d), lambda i, f: (f, 0)), vec]
    out_specs = [any_spec]
    out_shape = [jax.ShapeDtypeStruct((m, d), F32)]
    args = [x, g2, w_gate, w_up, w_down, gf]
    scratch = [pltpu.VMEM((tm, d), F32), pltpu.VMEM((tm, d), BF16),
               pltpu.SemaphoreType.DMA((FFN_TRANSFER_PARTS,)), pltpu.SemaphoreType.DMA((FFN_TRANSFER_PARTS,))]
    if rider is not None:
        in_specs.append(any_spec)
        out_specs.append(any_spec)
        out_shape.append(jax.ShapeDtypeStruct(rider.shape, F32))
        args.append(rider)
        scratch += [pltpu.VMEM(rider.shape, F32), pltpu.VMEM(rider.shape, BF16), pltpu.SemaphoreType.DMA((2,))]
    roll = None
    if caches is not None:
        cache_k, cache_v, qkv_new = caches
        b, buf, n_heads, _ = cache_k.shape
        n_new = qkv_new.shape[1]
        assert buf == MAX_WINDOW, "the rolled cache keeps exactly the window"
        old = buf - n_new
        steps = grid[0] * grid[1]
        per_seq = max(p for p in range(2, old + 1) if old % p == 0 and b * p + 2 <= steps)
        n_chunks = b * per_seq
        in_specs += [any_spec] * 3
        out_specs += [any_spec] * 2
        out_shape += [jax.ShapeDtypeStruct((b, buf, n_heads, HEAD_DIM), F32)] * 2
        args += [cache_k, cache_v, qkv_new]
        scratch += [pltpu.VMEM((2, 2, old // per_seq, n_heads, HEAD_DIM), F32),
                    pltpu.SemaphoreType.DMA((2, 2)), pltpu.SemaphoreType.DMA((2, 2)),
                    pltpu.SemaphoreType.DMA((2,))]
        roll = (n_chunks, per_seq, n_heads)
    return pl.pallas_call(
        functools.partial(_ffn_kernel, roll=roll, rider=rider is not None),
        grid=grid,
        in_specs=in_specs,
        out_specs=out_specs,
        out_shape=out_shape,
        scratch_shapes=scratch,
        compiler_params=_params("arbitrary", "arbitrary"),
        name="ffn",
    )(*args)


def kernel(x_prompt, x_sample, cache_k, cache_v, state_conv, norm1_g, w_in, conv_w, conv_b, conv_ln_g,
           conv_ln_b, w_out, norm2_g, w_gate, w_up, w_down, final_g):
    depth = w_in.shape[0]
    bp, s, d = x_prompt.shape
    bs, t_new, _ = x_sample.shape
    assert depth == 1 and bp == 1, "one layer, one prompt sequence"
    attn_w = d // 2
    conv_c = d - attn_w
    n_heads = attn_w // HEAD_DIM
    buf = cache_k.shape[2]

    row = lambda v: v.reshape(1, -1)
    l = 0
    w_in_b, w_out_b = w_in[l], w_out[l]
    g1, g2, gf = row(norm1_g[l]), row(norm2_g[l]), row(final_g)
    cb, lg, lb = row(conv_b[l]), row(conv_ln_g[l]), row(conv_ln_b[l])

    ms = bs * t_new
    xs = x_sample.reshape(ms, d)
    pos_s = PAST_LEN + jnp.tile(jnp.arange(t_new, dtype=jnp.int32), bs)
    qkv_s, u_s = _project_in(xs, g1, w_in_b, pos_s, attn_w, conv_c)
    qkv_s = qkv_s.reshape(bs, t_new, 3 * n_heads, HEAD_DIM)
    attn_s = _attn_sample(qkv_s, cache_k[l], cache_v[l])
    conv_s, nc_s = _conv_sample(state_conv[l], u_s.reshape(bs, t_new, conv_c), conv_w[l], cb, lg, lb)
    x1_s = _outproj(xs, attn_s.reshape(ms, attn_w), conv_s.reshape(ms, conv_c), w_out_b)

    xp = x_prompt.reshape(s, d)
    qkv_p, u_p = _project_in(xp, g1, w_in_b, jnp.arange(s, dtype=jnp.int32), attn_w, conv_c)
    attn_p, (w_gate_b, w_up_b, w_down_b) = _attn_prompt(qkv_p, n_heads, (w_gate[l], w_up[l], w_down[l]))
    conv_p = _conv_prompt(u_p, conv_w[l], cb, lg, lb)
    x1_p = _outproj(xp, attn_p, conv_p, w_out_b)
    y_p, y_s, nk_s, nv_s = _ffn(x1_p, g2, w_gate_b, w_up_b, w_down_b, gf, rider=x1_s,
                                caches=(cache_k[l], cache_v[l], qkv_s))
    keep_p = min(MAX_WINDOW, s)
    nk_p = qkv_p[s - keep_p:, attn_w:2 * attn_w].reshape(1, 1, keep_p, n_heads, HEAD_DIM)
    nv_p = qkv_p[s - keep_p:, 2 * attn_w:].reshape(1, 1, keep_p, n_heads, HEAD_DIM)
    n_state = CONV_K - 1
    assert s >= n_state
    nc_p = u_p[s - n_state:].reshape(1, 1, n_state, conv_c)

    return (y_p.reshape(1, s, d), y_s.reshape(bs, t_new, d), nk_p, nv_p, nc_p,
            nk_s[None], nv_s[None], nc_s.reshape(1, bs, n_state, conv_c))
```

```python
import functools
import math

import jax
import jax.numpy as jnp
from jax import lax
from jax.experimental import pallas as pl
from jax.experimental.pallas import tpu as pltpu

F32 = jnp.float32
BF16 = jnp.bfloat16

HEAD_DIM = 128
ROPE_DIM = HEAD_DIM // 4
ROPE_THETA = 500000.0
CONV_K = 31
RMS_EPS = 1e-6
LN_EPS = 1e-5
PAST_LEN = 8192
DILATIONS = (1, 4, 16)
SPAN = 128
MAX_WINDOW = 2048
ATTN_TILE = MAX_WINDOW
ATTN_UNROLL = 16
CONV_HALO = 32
FFN_DOWN_CHUNK = 1024
NORM_ROWS = 64
FFN_TRANSFER_PARTS = 4
WEIGHT_CAST_ROWS = 512

LANES = 128
VMEM_LIMIT_BYTES = 56 * 1024 * 1024


def _params(*sem):
    return pltpu.CompilerParams(dimension_semantics=sem, vmem_limit_bytes=VMEM_LIMIT_BYTES)


def _tile(n, pref, mult=8):
    if n <= pref:
        return n
    for t in range(pref - pref % mult, 0, -mult):
        if n % t == 0:
            return t
    raise ValueError(f"no tile for {n}")


def _rms_to_bf16(x, g):
    ms = jnp.mean(x * x, axis=-1, keepdims=True)
    return (x * lax.rsqrt(ms + RMS_EPS) * g).astype(BF16)


def _dot(a, b):
    return jnp.dot(a, b, preferred_element_type=F32)


def _dot_nt(a, b):
    return lax.dot_general(a, b, (((1,), (1,)), ((), ())), preferred_element_type=F32)


def _norm_kernel(x_ref, g_ref, h_ref):
    h_ref[...] = _rms_to_bf16(x_ref[...], g_ref[...])


def _cast_weight(w_ref, w_bf):
    k = w_ref.shape[0]
    rc = _tile(k, WEIGHT_CAST_ROWS)
    for r0 in range(0, k, rc):
        w_bf[r0:r0 + rc, :] = w_ref[r0:r0 + rc, :].astype(BF16)


def _row_parts(tm):
    half = tm // 2
    return ((0, tm),) if half % 16 else ((0, half), (half, tm))


def _qkv_kernel(h_ref, w_ref, tab_ref, o_ref, w_bf):
    @pl.when(pl.program_id(1) == 0)
    def _():
        _cast_weight(w_ref, w_bf)

    for r0, r1 in _row_parts(h_ref.shape[0]):
        z = _dot(h_ref[r0:r1, :], w_bf[...])
        c, s1, s2 = tab_ref[0, 0, r0:r1, :], tab_ref[0, 1, r0:r1, :], tab_ref[0, 2, r0:r1, :]
        for j in range(z.shape[1] // HEAD_DIM):
            zj = z[:, j * HEAD_DIM:(j + 1) * HEAD_DIM]
            lo = pltpu.roll(zj, ROPE_DIM // 2, 1)
            hi = pltpu.roll(zj, HEAD_DIM - ROPE_DIM // 2, 1)
            o_ref[r0:r1, j * HEAD_DIM:(j + 1) * HEAD_DIM] = zj * c + lo * s1 + hi * s2


def _glu_kernel(h_ref, wa_ref, wg_ref, *rest, rider):
    if rider:
        hs_ref, u_ref, us_ref, wa_bf, wg_bf = rest
    else:
        u_ref, wa_bf, wg_bf = rest

    def glu(h):
        return _dot(h, wa_bf[...]) * jax.nn.sigmoid(_dot(h, wg_bf[...]))

    @pl.when(pl.program_id(1) == 0)
    def _():
        _cast_weight(wa_ref, wa_bf)
        _cast_weight(wg_ref, wg_bf)
        if rider:
            us_ref[...] = glu(hs_ref[...])

    for r0, r1 in _row_parts(h_ref.shape[0]):
        u_ref[r0:r1, :] = glu(h_ref[r0:r1, :])


def _rope_tables(pos):
    half = ROPE_DIM // 2
    inv = ROPE_THETA ** (-jnp.arange(half, dtype=F32) / half)
    ang = pos.astype(F32)[:, None] * inv[None, :]
    cos, sin = jnp.cos(ang), jnp.sin(ang)
    m = pos.shape[0]
    one = jnp.ones((m, HEAD_DIM - ROPE_DIM), F32)
    zero = jnp.zeros((m, HEAD_DIM - ROPE_DIM), F32)
    zh = jnp.zeros((m, half), F32)
    c = jnp.concatenate([cos, cos, one], axis=1)
    s1 = jnp.concatenate([zh, sin, zero], axis=1)
    s2 = jnp.concatenate([-sin, zh, zero], axis=1)
    rot = jnp.stack([c, s1, s2])
    ident = jnp.stack([jnp.ones_like(c), jnp.zeros_like(c), jnp.zeros_like(c)])
    return jnp.stack([rot, ident])


def _norm1(x, g1):
    m, d = x.shape
    tr = _tile(m, 256)
    return pl.pallas_call(
        _norm_kernel,
        grid=(m // tr,),
        in_specs=[pl.BlockSpec((tr, d), lambda i: (i, 0)), pl.BlockSpec((1, d), lambda i: (0, 0))],
        out_specs=pl.BlockSpec((tr, d), lambda i: (i, 0)),
        out_shape=jax.ShapeDtypeStruct((m, d), BF16),
        compiler_params=_params("parallel"),
        name="norm1",
    )(x, g1)


def _weight_mode(m, tm):
    return pl.Buffered(1) if m // tm > 1 else None


def _qkv_proj(h, w_in, pos, attn_w):
    m, d = h.shape
    tm = _tile(m, 1024, 16)
    tn = _tile(attn_w, 1024, LANES)
    once = _weight_mode(m, tm)
    n_rope = 2 * attn_w // tn
    return pl.pallas_call(
        _qkv_kernel,
        grid=(3 * attn_w // tn, m // tm),
        in_specs=[pl.BlockSpec((tm, d), lambda n, i: (i, 0)),
                  pl.BlockSpec((d, tn), lambda n, i: (0, n), pipeline_mode=once),
                  pl.BlockSpec((1, 3, tm, HEAD_DIM), lambda n, i: (jnp.where(n < n_rope, 0, 1), 0, i, 0))],
        out_specs=pl.BlockSpec((tm, tn), lambda n, i: (i, n)),
        out_shape=jax.ShapeDtypeStruct((m, 3 * attn_w), F32),
        scratch_shapes=[pltpu.VMEM((d, tn), BF16)],
        compiler_params=_params("arbitrary", "arbitrary"),
        name="qkv_proj",
    )(h, w_in, _rope_tables(pos))


def _glu_proj(h, w_in, attn_w, conv_c, rider=None):
    m, d = h.shape
    tm = _tile(m, 1024, 16)
    tc = _tile(conv_c, 512, LANES)
    once = _weight_mode(m, tm)
    a0 = 3 * attn_w // tc
    g0 = (3 * attn_w + conv_c) // tc
    in_specs = [pl.BlockSpec((tm, d), lambda n, i: (i, 0)),
                pl.BlockSpec((d, tc), lambda n, i: (0, a0 + n), pipeline_mode=once),
                pl.BlockSpec((d, tc), lambda n, i: (0, g0 + n), pipeline_mode=once)]
    out_specs = [pl.BlockSpec((tm, tc), lambda n, i: (i, n))]
    out_shape = [jax.ShapeDtypeStruct((m, conv_c), F32)]
    args = [h, w_in, w_in]
    if rider is not None:
        ms = rider.shape[0]
        in_specs.append(pl.BlockSpec((ms, d), lambda n, i: (0, 0)))
        out_specs.append(pl.BlockSpec((ms, tc), lambda n, i: (0, n)))
        out_shape.append(jax.ShapeDtypeStruct((ms, conv_c), F32))
        args.append(rider)
    return pl.pallas_call(
        functools.partial(_glu_kernel, rider=rider is not None),
        grid=(conv_c // tc, m // tm),
        in_specs=in_specs,
        out_specs=out_specs,
        out_shape=out_shape,
        scratch_shapes=[pltpu.VMEM((d, tc), BF16)] * 2,
        compiler_params=_params("arbitrary", "arbitrary"),
        name="glu_proj",
    )(*args)


def _rows(start, size, stride):
    return pl.ds(start, size) if stride == 1 else pl.ds(start, size, stride=stride)


def _attn_prompt_kernel(q_ref, kp_ref, kc_ref, vp_ref, vc_ref, *rest):
    n_w = (len(rest) - 4) // 2
    w_in, o_ref, w_out = rest[:n_w], rest[n_w], rest[n_w + 1:2 * n_w + 1]
    num_scr, m_scr, l_scr = rest[2 * n_w + 1:]
    for src, dst in zip(w_in, w_out):
        rc = _tile(src.shape[0], 16, 16)
        for r0 in range(0, src.shape[0], rc):
            dst[r0:r0 + rc, :] = src[r0:r0 + rc, :].astype(BF16)

    tile = pl.program_id(1)
    t = q_ref.shape[0]
    scale = HEAD_DIM ** -0.5
    qi = lax.broadcasted_iota(jnp.int32, (SPAN, 2 * SPAN), 0)
    kj = lax.broadcasted_iota(jnp.int32, (SPAN, 2 * SPAN), 1)
    band = (kj >= qi) & (kj <= qi + SPAN)
    first_key = jnp.where(tile > 0, 0, SPAN)
    band_first = band & (kj >= first_key)

    def attend(gi, d, q_start, q, k, v, mask):
        s = _dot_nt(q.astype(BF16), k.astype(BF16)) * scale
        s = jnp.where(mask, s, -jnp.inf)
        m = jnp.max(s, axis=-1, keepdims=True)
        p = jnp.exp(s - m)
        l = jnp.sum(p, axis=-1, keepdims=True)
        num = _dot(p.astype(BF16), v.astype(BF16))
        rows = _rows(q_start, SPAN, d)
        num_scr[gi, rows, :] = num
        m_scr[gi, rows, :] = jnp.broadcast_to(m, (SPAN, HEAD_DIM))
        l_scr[gi, rows, :] = jnp.broadcast_to(l, (SPAN, HEAD_DIM))

    def first_block(gi, d, r):
        q = q_ref[_rows(r, SPAN, d), :]
        k = jnp.concatenate([kp_ref[_rows(t - d * SPAN + r, SPAN, d), :],
                             kc_ref[_rows(r, SPAN, d), :]], axis=0)
        v = jnp.concatenate([vp_ref[_rows(t - d * SPAN + r, SPAN, d), :],
                             vc_ref[_rows(r, SPAN, d), :]], axis=0)
        attend(gi, d, r, q, k, v, band_first)

    def later_block(gi, d, r, b):
        q_start = r + d * SPAN * b
        if d == 1:
            q_start = pl.multiple_of(q_start, SPAN)
        k_rows = _rows(q_start - d * SPAN, 2 * SPAN, d)
        attend(gi, d, q_start, q_ref[_rows(q_start, SPAN, d), :], kc_ref[k_rows, :], vc_ref[k_rows, :], band)

    u = ATTN_UNROLL
    for gi, d in enumerate(DILATIONS):
        n_blocks = t // (d * SPAN)
        if n_blocks >= u:
            assert n_blocks % u == 0

            def class_body(r, carry, gi=gi, d=d, n_blocks=n_blocks):
                first_block(gi, d, r)
                for b in range(1, u):
                    later_block(gi, d, r, b)

                def rest(i, c):
                    for j in range(u):
                        later_block(gi, d, r, i * u + j)
                    return c

                if n_blocks > u:
                    lax.fori_loop(1, n_blocks // u, rest, 0)
                return carry

            if d == 1:
                class_body(0, 0)
            else:
                lax.fori_loop(0, d, class_body, 0)
        else:
            per_iter = u // n_blocks
            assert u % n_blocks == 0 and d % per_iter == 0

            def class_group(i, carry, gi=gi, d=d, n_blocks=n_blocks, per_iter=per_iter):
                for j in range(per_iter):
                    r = i * per_iter + j
                    first_block(gi, d, r)
                    for b in range(1, n_blocks):
                        later_block(gi, d, r, b)
                return carry

            lax.fori_loop(0, d // per_iter, class_group, 0)

    def combine(c, carry):
        rows = pl.ds(pl.multiple_of(c * SPAN, SPAN), SPAN)
        ms = [m_scr[g, rows, :] for g in range(len(DILATIONS))]
        m_all = functools.reduce(jnp.maximum, ms)
        num = jnp.zeros((SPAN, HEAD_DIM), F32)
        den = jnp.zeros((SPAN, HEAD_DIM), F32)
        for g in range(len(DILATIONS)):
            sc = jnp.exp(ms[g] - m_all)
            num = num + num_scr[g, rows, :] * sc
            den = den + l_scr[g, rows, :] * sc
        o_ref[rows, :] = (num / den).astype(o_ref.dtype)
        return carry

    lax.fori_loop(0, t // SPAN, combine, 0)


def _attn_prompt(qkv, n_heads, weights=()):
    s = qkv.shape[0]
    t = ATTN_TILE
    assert s % t == 0, "prompt length must be a multiple of the dilation tile"
    n_tiles = s // t
    cur = lambda off: pl.BlockSpec((t, HEAD_DIM), lambda h, i: (i, off + h))
    prev = lambda off: pl.BlockSpec((t, HEAD_DIM), lambda h, i: (jnp.maximum(i - 1, 0), off + h))
    w_specs, w_shapes = [], []
    for w in weights:
        rows, cols = w.shape
        rb = next(r for r in range(16, rows + 1, 16) if rows % r == 0 and rows // r <= n_heads * n_tiles)
        spec = pl.BlockSpec((rb, cols), lambda h, i, nb=rows // rb: (jnp.minimum(h * n_tiles + i, nb - 1), 0))
        w_specs.append(spec)
        w_shapes.append(jax.ShapeDtypeStruct(w.shape, BF16))
    out = pl.pallas_call(
        _attn_prompt_kernel,
        grid=(n_heads, n_tiles),
        in_specs=[cur(0), prev(n_heads), cur(n_heads), prev(2 * n_heads), cur(2 * n_heads)] + w_specs,
        out_specs=[pl.BlockSpec((t, HEAD_DIM), lambda h, i: (i, h))] + w_specs,
        out_shape=[jax.ShapeDtypeStruct((s, n_heads * HEAD_DIM), BF16)] + w_shapes,
        scratch_shapes=[pltpu.VMEM((len(DILATIONS), t, HEAD_DIM), F32)] * 3,
        compiler_params=_params("arbitrary", "arbitrary"),
        name="attn_prompt",
    )(qkv, qkv, qkv, qkv, qkv, *weights)
    return out[0], tuple(out[1:])


def _attn_sample_kernel(qkv_ref, kt_ref, vt_ref, kf_ref, vf_ref, o_ref, kn_scr, vn_scr, *, n_heads):
    n_new = qkv_ref.shape[1]
    tail = kt_ref.shape[1]
    d_far = DILATIONS[-1]
    kn_scr[0:tail] = kt_ref[0]
    vn_scr[0:tail] = vt_ref[0]
    kn_scr[tail:tail + n_new] = qkv_ref[0, :, n_heads:2 * n_heads, :]
    vn_scr[tail:tail + n_new] = qkv_ref[0, :, 2 * n_heads:3 * n_heads, :]
    n_near_far = tail // d_far + 1

    for i in range(n_new):
        q = qkv_ref[0, i, 0:n_heads, :] * (HEAD_DIM ** -0.5)
        lists = []
        for d in DILATIONS[:-1]:
            rows = _rows(tail + i - d * SPAN, SPAN + 1, d)
            lists.append((kn_scr[rows], vn_scr[rows]))
        rows = _rows(tail + i - d_far * (n_near_far - 1), n_near_far, d_far)
        lists.append((kn_scr[rows], vn_scr[rows]))
        lists.append((kf_ref[0, :, i], vf_ref[0, :, i]))
        scores = [jnp.sum(k * q[None], axis=-1, keepdims=True) for k, _ in lists]
        m = functools.reduce(jnp.maximum, [jnp.max(s, axis=0) for s in scores])
        den = jnp.zeros((n_heads, 1), F32)
        num = jnp.zeros((n_heads, HEAD_DIM), F32)
        for s, (_, v) in zip(scores, lists):
            p = jnp.exp(s - m[None])
            den = den + jnp.sum(p, axis=0)
            num = num + jnp.sum(p * v, axis=0)
        o_ref[0, i] = num / den


def _attn_sample(qkv, cache_k, cache_v):
    b, n_new, _, _ = qkv.shape
    buf, n_heads = cache_k.shape[1], cache_k.shape[2]
    d_mid, d_far = DILATIONS[1], DILATIONS[2]
    tail = d_mid * SPAN
    assert buf == d_far * SPAN and n_new <= d_mid and tail % d_far == 0
    n_far = (buf - tail) // d_far
    far_shape = (b, buf // d_far, d_far, n_heads, HEAD_DIM)
    near = pl.BlockSpec((1, tail, n_heads, HEAD_DIM), lambda i: (i, buf // tail - 1, 0, 0))
    far = pl.BlockSpec((1, n_far, n_new, n_heads, HEAD_DIM), lambda i: (i, 0, 0, 0, 0))
    new = pl.BlockSpec((1, n_new, 3 * n_heads, HEAD_DIM), lambda i: (i, 0, 0, 0))
    ctx = pltpu.VMEM((tail + 8, n_heads, HEAD_DIM), F32)
    return pl.pallas_call(
        functools.partial(_attn_sample_kernel, n_heads=n_heads),
        grid=(b,),
        in_specs=[new, near, near, far, far],
        out_specs=pl.BlockSpec((1, n_new, n_heads, HEAD_DIM), lambda i: (i, 0, 0, 0)),
        out_shape=jax.ShapeDtypeStruct((b, n_new, n_heads, HEAD_DIM), F32),
        scratch_shapes=[ctx, ctx],
        compiler_params=_params("parallel"),
        name="attn_sample",
    )(qkv, cache_k, cache_v, cache_k.reshape(far_shape), cache_v.reshape(far_shape))


def _ln_swish(y, g, b):
    mu = jnp.mean(y, axis=-1, keepdims=True)
    yc = y - mu
    var = jnp.mean(yc * yc, axis=-1, keepdims=True)
    z = yc * lax.rsqrt(var + LN_EPS) * g + b
    return z * jax.nn.sigmoid(z)


def _conv_prompt_kernel(up_ref, u_ref, w_ref, b_ref, lg_ref, lb_ref, o_ref, ctx_scr, y_scr):
    tm, c = u_ref.shape
    n_slabs = c // LANES
    slab = lambda cs: slice(cs * LANES, (cs + 1) * LANES)

    @pl.when(pl.program_id(0) > 0)
    def _():
        for cs in range(n_slabs):
            ctx_scr[cs, 0:CONV_HALO, :] = up_ref[:, slab(cs)]

    @pl.when(pl.program_id(0) == 0)
    def _():
        for cs in range(n_slabs):
            ctx_scr[cs, 0:CONV_HALO, :] = jnp.zeros((CONV_HALO, LANES), F32)

    for cs in range(n_slabs):
        ctx_scr[cs, CONV_HALO:CONV_HALO + tm, :] = u_ref[:, slab(cs)]

    off = CONV_HALO - (CONV_K - 1)
    rc = 32

    for cs in range(n_slabs):
        taps = [jnp.broadcast_to(w_ref[j:j + 1, slab(cs)], (rc, LANES)) for j in range(CONV_K)]
        bias = jnp.broadcast_to(b_ref[:, slab(cs)], (rc, LANES))

        def chunk(i, carry, cs=cs, taps=taps, bias=bias):
            r0 = i * (2 * rc)
            for phase in range(2):
                acc = bias
                for j in range(CONV_K):
                    acc = acc + ctx_scr[cs, pl.ds(r0 + phase + off + j, rc, stride=2), :] * taps[j]
                y_scr[cs, pl.ds(r0 + phase, rc, stride=2), :] = acc
            return carry

        lax.fori_loop(0, tm // (2 * rc), chunk, 0)

    ln_rows = 16
    ln_unroll = 4

    def ln_rows_at(row0):
        rows = pl.ds(pl.multiple_of(row0, ln_rows), ln_rows)
        ys = [y_scr[cs, rows, :] for cs in range(n_slabs)]
        mu = jnp.sum(functools.reduce(jnp.add, ys), axis=-1, keepdims=True) * (1.0 / c)
        ycs = [y - mu for y in ys]
        var = jnp.sum(functools.reduce(jnp.add, [yc * yc for yc in ycs]), axis=-1, keepdims=True) * (1.0 / c)
        inv = lax.rsqrt(var + LN_EPS)
        for cs in range(n_slabs):
            z = ycs[cs] * inv * lg_ref[:, slab(cs)] + lb_ref[:, slab(cs)]
            o_ref[rows, slab(cs)] = (z * jax.nn.sigmoid(z)).astype(o_ref.dtype)

    def ln_chunk(i, carry):
        for j in range(ln_unroll):
            ln_rows_at((i * ln_unroll + j) * ln_rows)
        return carry

    lax.fori_loop(0, tm // (ln_rows * ln_unroll), ln_chunk, 0)


def _conv_prompt(u, conv_w, conv_b, ln_g, ln_b):
    s, c = u.shape
    tm = _tile(s, 256, CONV_HALO)
    vec = pl.BlockSpec((1, c), lambda i: (0, 0))
    return pl.pallas_call(
        _conv_prompt_kernel,
        grid=(s // tm,),
        in_specs=[pl.BlockSpec((CONV_HALO, c), lambda i: (jnp.maximum(i * (tm // CONV_HALO) - 1, 0), 0)),
                  pl.BlockSpec((tm, c), lambda i: (i, 0)),
                  pl.BlockSpec((CONV_K, c), lambda i: (0, 0)), vec, vec, vec],
        out_specs=pl.BlockSpec((tm, c), lambda i: (i, 0)),
        out_shape=jax.ShapeDtypeStruct((s, c), BF16),
        scratch_shapes=[pltpu.VMEM((c // LANES, CONV_HALO + tm, LANES), F32),
                        pltpu.VMEM((c // LANES, tm, LANES), F32)],
        compiler_params=_params("parallel"),
        name="conv_prompt",
    )(u, u, conv_w, conv_b, ln_g, ln_b)


def _conv_sample_kernel(st_ref, u_ref, w_ref, b_ref, lg_ref, lb_ref, o_ref, ns_ref, ctx_scr):
    n_state, n_new = st_ref.shape[1], u_ref.shape[1]
    ctx_scr[0:n_state, :] = st_ref[0]
    ctx_scr[n_state:n_state + n_new, :] = u_ref[0]
    acc = jnp.zeros(u_ref.shape[1:], F32)
    for j in range(CONV_K):
        acc = acc + ctx_scr[j:j + n_new, :] * w_ref[j:j + 1, :]
    o_ref[0] = _ln_swish(acc + b_ref[...], lg_ref[...], lb_ref[...])
    ns_ref[0] = ctx_scr[n_new:n_new + n_state, :]


def _conv_sample(state, u, conv_w, conv_b, ln_g, ln_b):
    b, n_state, c = state.shape
    n_new = u.shape[1]
    assert n_state == CONV_K - 1
    vec = pl.BlockSpec((1, c), lambda i: (0, 0))
    return pl.pallas_call(
        _conv_sample_kernel,
        grid=(b,),
        in_specs=[pl.BlockSpec((1, n_state, c), lambda i: (i, 0, 0)),
                  pl.BlockSpec((1, n_new, c), lambda i: (i, 0, 0)),
                  pl.BlockSpec((CONV_K, c), lambda i: (0, 0)), vec, vec, vec],
        out_specs=[pl.BlockSpec((1, n_new, c), lambda i: (i, 0, 0)),
                   pl.BlockSpec((1, n_state, c), lambda i: (i, 0, 0))],
        out_shape=[jax.ShapeDtypeStruct((b, n_new, c), F32),
                   jax.ShapeDtypeStruct((b, n_state, c), F32)],
        scratch_shapes=[pltpu.VMEM((n_state + n_new, c), F32)],
        compiler_params=_params("parallel"),
        name="conv_sample",
    )(state, u, conv_w, conv_b, ln_g, ln_b)


def _outproj_kernel(x_ref, a_ref, c_ref, wa_ref, wc_ref, *rest, rider):
    if rider:
        xs_ref, as_ref, cs_ref, o_ref, os_ref, wa_bf, wc_bf = rest
    else:
        o_ref, wa_bf, wc_bf = rest

    def project(x, a, c):
        return x + (_dot(a.astype(BF16), wa_bf[...]) + _dot(c.astype(BF16), wc_bf[...]))

    @pl.when(pl.program_id(1) == 0)
    def _():
        _cast_weight(wa_ref, wa_bf)
        _cast_weight(wc_ref, wc_bf)
        if rider:
            os_ref[...] = project(xs_ref[...], as_ref[...], cs_ref[...])

    for r0, r1 in _row_parts(x_ref.shape[0]):
        o_ref[r0:r1, :] = project(x_ref[r0:r1, :], a_ref[r0:r1, :], c_ref[r0:r1, :])


def _outproj(x, attn, conv, w_out, rider=None):
    m, d = x.shape
    a, c = attn.shape[1], conv.shape[1]
    assert a == c
    tm = _tile(m, 1024, 16)
    tn = _tile(d, 512, LANES)
    in_specs = [pl.BlockSpec((tm, tn), lambda n, i: (i, n)),
                pl.BlockSpec((tm, a), lambda n, i: (i, 0)),
                pl.BlockSpec((tm, c), lambda n, i: (i, 0)),
                pl.BlockSpec((a, tn), lambda n, i: (0, n)),
                pl.BlockSpec((c, tn), lambda n, i: (1, n))]
    out_specs = [pl.BlockSpec((tm, tn), lambda n, i: (i, n))]
    out_shape = [jax.ShapeDtypeStruct((m, d), F32)]
    args = [x, attn, conv, w_out, w_out]
    if rider is not None:
        xs, attn_s, conv_s = rider
        ms = xs.shape[0]
        in_specs += [pl.BlockSpec((ms, tn), lambda n, i: (0, n)),
                     pl.BlockSpec((ms, a), lambda n, i: (0, 0)),
                     pl.BlockSpec((ms, c), lambda n, i: (0, 0))]
        out_specs.append(pl.BlockSpec((ms, tn), lambda n, i: (0, n)))
        out_shape.append(jax.ShapeDtypeStruct((ms, d), F32))
        args += [xs, attn_s, conv_s]
    return pl.pallas_call(
        functools.partial(_outproj_kernel, rider=rider is not None),
        grid=(d // tn, m // tm),
        in_specs=in_specs,
        out_specs=out_specs,
        out_shape=out_shape,
        scratch_shapes=[pltpu.VMEM((a, tn), BF16), pltpu.VMEM((c, tn), BF16)],
        compiler_params=_params("arbitrary", "arbitrary"),
        name="out_proj",
    )(*args)


def _ffn_kernel(*refs, roll, rider):
    refs = list(refs)
    take = lambda n: [refs.pop(0) for _ in range(n)]
    x_hbm, g2_ref, wg_ref, wu_ref, wd_ref, gf_ref = take(6)
    xs_hbm, = take(1) if rider else (None,)
    caches = take(3) if roll else None
    y_hbm, = take(1)
    ys_hbm, = take(1) if rider else (None,)
    rolled = take(2) if roll else None
    acc_ref, h_ref, x_sem, y_sem = take(4)
    acc_s, h_s, s_sem = take(3) if rider else (None, None, None)
    i, f = pl.program_id(0), pl.program_id(1)
    last_f = pl.num_programs(1) - 1
    if roll:
        slots, in_sem, out_sem, new_sem = take(4)
        _cache_roll_step(i * pl.num_programs(1) + f, roll, caches[:2], caches[2], rolled,
                         slots, in_sem, out_sem, new_sem)
    tm, d = acc_ref.shape
    parts = FFN_TRANSFER_PARTS if tm % (16 * FFN_TRANSFER_PARTS) == 0 else 1
    tp = tm // parts

    def x_copy(t, c):
        return pltpu.make_async_copy(x_hbm.at[pl.ds(pl.multiple_of(t * tm + c * tp, tp), tp)],
                                     acc_ref.at[pl.ds(c * tp, tp)], x_sem.at[c])

    def y_copy(t, c):
        return pltpu.make_async_copy(acc_ref.at[pl.ds(c * tp, tp)],
                                     y_hbm.at[pl.ds(pl.multiple_of(t * tm + c * tp, tp), tp)], y_sem.at[c])

    def by_row_chunks(start, size, body):
        rc = _tile(size, NORM_ROWS)

        def step(c, carry):
            body(pl.ds(pl.multiple_of(start + c * rc, rc), rc))
            return carry
        lax.fori_loop(0, size // rc, step, 0)

    def norm_in(acc, h):
        def body(rows):
            h[rows, :] = _rms_to_bf16(acc[rows, :], g2_ref[...])
        return body

    def norm_out(acc):
        def body(rows):
            y = acc[rows, :]
            ms = jnp.mean(y * y, axis=-1, keepdims=True)
            acc[rows, :] = y * lax.rsqrt(ms + RMS_EPS) * gf_ref[...]
        return body

    def swiglu_into(acc, h_rows):
        h = h_rows[...]
        gate = _dot(h, wg_ref[...])
        act = (gate * jax.nn.sigmoid(gate) * _dot(h, wu_ref[...])).astype(BF16)
        dc = _tile(d, FFN_DOWN_CHUNK, LANES)
        for c0 in range(0, d, dc):
            acc[:, c0:c0 + dc] += _dot(act, wd_ref[:, c0:c0 + dc])

    @pl.when(f == 0)
    def _():
        for c in range(parts):
            @pl.when(i > 0)
            def _(c=c):
                y_copy(i - 1, c).wait()
            x_copy(i, c).start()
        for c in range(parts):
            x_copy(i, c).wait()
            by_row_chunks(c * tp, tp, norm_in(acc_ref, h_ref))
        if rider:
            @pl.when(i == 0)
            def _():
                cp = pltpu.make_async_copy(xs_hbm, acc_s, s_sem.at[0])
                cp.start()
                cp.wait()
                by_row_chunks(0, acc_s.shape[0], norm_in(acc_s, h_s))

    swiglu_into(acc_ref, h_ref)
    if rider:
        @pl.when(i == 0)
        def _():
            swiglu_into(acc_s, h_s)

    @pl.when(f == last_f)
    def _():
        for c in range(parts):
            by_row_chunks(c * tp, tp, norm_out(acc_ref))
            y_copy(i, c).start()

        @pl.when(i == pl.num_programs(0) - 1)
        def _():
            for c in range(parts):
                y_copy(i, c).wait()

        if rider:
            @pl.when(i == 0)
            def _():
                by_row_chunks(0, acc_s.shape[0], norm_out(acc_s))
                cp = pltpu.make_async_copy(acc_s, ys_hbm, s_sem.at[1])
                cp.start()
                cp.wait()


def _cache_roll_step(step, roll, caches, new_ref, rolled, slots, in_sem, out_sem, new_sem):
    n_chunks, per_seq, n_heads = roll
    rows = slots.shape[2]
    n_new = new_ref.shape[1]

    def chunk_copies(c):
        seq, j, slot = c // per_seq, c % per_seq, c % 2
        pairs = []
        for i in range(2):
            buf = slots.at[slot, i]
            pairs.append((
                pltpu.make_async_copy(caches[i].at[seq, pl.ds(j * rows + n_new, rows)], buf, in_sem.at[slot, i]),
                pltpu.make_async_copy(buf, rolled[i].at[seq, pl.ds(j * rows, rows)], out_sem.at[slot, i])))
        return pairs

    def new_copies(c):
        seq = c // per_seq
        return [pltpu.make_async_copy(new_ref.at[seq, :, pl.ds((i + 1) * n_heads, n_heads)],
                                      rolled[i].at[seq, pl.ds(per_seq * rows, n_new)], new_sem.at[i])
                for i in range(2)]

    @pl.when((step >= 2) & (step < n_chunks + 2))
    def _():
        for _, out in chunk_copies(step - 2):
            out.wait()

    @pl.when(step < n_chunks)
    def _():
        for inp, _ in chunk_copies(step):
            inp.start()

    @pl.when((step >= 1) & (step < n_chunks + 1))
    def _():
        for inp, out in chunk_copies(step - 1):
            inp.wait()
            out.start()

    @pl.when((step >= 1) & (step < n_chunks + 1) & ((step - 1) % per_seq == 0))
    def _():
        for cp in new_copies(step - 1):
            cp.wait()

    @pl.when((step < n_chunks) & (step % per_seq == 0))
    def _():
        for cp in new_copies(step):
            cp.start()


def _ffn(x, g2, w_gate, w_up, w_down, gf, rider=None, caches=None):
    m, d = x.shape
    dff = w_gate.shape[1]
    tm = _tile(m, 1024, 16)
    tf = _tile(dff, 256, LANES)
    nf = dff // tf
    grid = (m // tm, nf)
    vec = pl.BlockSpec((1, d), lambda i, f: (0, 0))
    any_spec = pl.BlockSpec(memory_space=pl.ANY)
    in_specs = [any_spec, vec,
                pl.BlockSpec((d, tf), lambda i, f: (0, f)),
                pl.BlockSpec((d, tf), lambda i, f: (0, f)),
                pl.BlockSpec((tf, d), lambda i, f: (f, 0)), vec]
    out_specs = [any_spec]
    out_shape = [jax.ShapeDtypeStruct((m, d), F32)]
    args = [x, g2, w_gate, w_up, w_down, gf]
    scratch = [pltpu.VMEM((tm, d), F32), pltpu.VMEM((tm, d), BF16),
               pltpu.SemaphoreType.DMA((FFN_TRANSFER_PARTS,)), pltpu.SemaphoreType.DMA((FFN_TRANSFER_PARTS,))]
    if rider is not None:
        in_specs.append(any_spec)
        out_specs.append(any_spec)
        out_shape.append(jax.ShapeDtypeStruct(rider.shape, F32))
        args.append(rider)
        scratch += [pltpu.VMEM(rider.shape, F32), pltpu.VMEM(rider.shape, BF16), pltpu.SemaphoreType.DMA((2,))]
    roll = None
    if caches is not None:
        cache_k, cache_v, qkv_new = caches
        b, buf, n_heads, _ = cache_k.shape
        n_new = qkv_new.shape[1]
        assert buf == MAX_WINDOW, "the rolled cache keeps exactly the window"
        old = buf - n_new
        steps = grid[0] * grid[1]
        per_seq = max(p for p in range(2, old + 1) if old % p == 0 and b * p + 2 <= steps)
        n_chunks = b * per_seq
        in_specs += [any_spec] * 3
        out_specs += [any_spec] * 2
        out_shape += [jax.ShapeDtypeStruct((b, buf, n_heads, HEAD_DIM), F32)] * 2
        args += [cache_k, cache_v, qkv_new]
        scratch += [pltpu.VMEM((2, 2, old // per_seq, n_heads, HEAD_DIM), F32),
                    pltpu.SemaphoreType.DMA((2, 2)), pltpu.SemaphoreType.DMA((2, 2)),
                    pltpu.SemaphoreType.DMA((2,))]
        roll = (n_chunks, per_seq, n_heads)
    return pl.pallas_call(
        functools.partial(_ffn_kernel, roll=roll, rider=rider is not None),
        grid=grid,
        in_specs=in_specs,
        out_specs=out_specs,
        out_shape=out_shape,
        scratch_shapes=scratch,
        compiler_params=_params("arbitrary", "arbitrary"),
        name="ffn",
    )(*args)


def kernel(x_prompt, x_sample, cache_k, cache_v, state_conv, norm1_g, w_in, conv_w, conv_b, conv_ln_g,
           conv_ln_b, w_out, norm2_g, w_gate, w_up, w_down, final_g):
    depth = w_in.shape[0]
    bp, s, d = x_prompt.shape
    bs, t_new, _ = x_sample.shape
    assert depth == 1 and bp == 1, "one layer, one prompt sequence"
    attn_w = d // 2
    conv_c = d - attn_w
    n_heads = attn_w // HEAD_DIM
    buf = cache_k.shape[2]

    row = lambda v: v.reshape(1, -1)
    l = 0
    w_in_b, w_out_b = w_in[l], w_out[l]
    g1, g2, gf = row(norm1_g[l]), row(norm2_g[l]), row(final_g)
    cb, lg, lb = row(conv_b[l]), row(conv_ln_g[l]), row(conv_ln_b[l])

    ms = bs * t_new
    xs = x_sample.reshape(ms, d)
    pos_s = PAST_LEN + jnp.tile(jnp.arange(t_new, dtype=jnp.int32), bs)
    h_s = _norm1(xs, g1)
    qkv_s = _qkv_proj(h_s, w_in_b, pos_s, attn_w).reshape(bs, t_new, 3 * n_heads, HEAD_DIM)
    attn_s = _attn_sample(qkv_s, cache_k[l], cache_v[l])

    xp = x_prompt.reshape(s, d)
    h_p = _norm1(xp, g1)
    qkv_p = _qkv_proj(h_p, w_in_b, jnp.arange(s, dtype=jnp.int32), attn_w)
    u_p, u_s = _glu_proj(h_p, w_in_b, attn_w, conv_c, rider=h_s)
    conv_s, nc_s = _conv_sample(state_conv[l], u_s.reshape(bs, t_new, conv_c), conv_w[l], cb, lg, lb)
    attn_p, (w_gate_b, w_up_b, w_down_b) = _attn_prompt(qkv_p, n_heads, (w_gate[l], w_up[l], w_down[l]))
    conv_p = _conv_prompt(u_p, conv_w[l], cb, lg, lb)
    x1_p, x1_s = _outproj(xp, attn_p, conv_p, w_out_b,
                          rider=(xs, attn_s.reshape(ms, attn_w), conv_s.reshape(ms, conv_c)))
    y_p, y_s, nk_s, nv_s = _ffn(x1_p, g2, w_gate_b, w_up_b, w_down_b, gf, rider=x1_s,
                                caches=(cache_k[l], cache_v[l], qkv_s))
    keep_p = min(MAX_WINDOW, s)
    nk_p = qkv_p[s - keep_p:, attn_w:2 * attn_w].reshape(1, 1, keep_p, n_heads, HEAD_DIM)
    nv_p = qkv_p[s - keep_p:, 2 * attn_w:].reshape(1, 1, keep_p, n_heads, HEAD_DIM)
    n_state = CONV_K - 1
    assert s >= n_state
    nc_p = u_p[s - n_state:].reshape(1, 1, n_state, conv_c)

    return (y_p.reshape(1, s, d), y_s.reshape(bs, t_new, d), nk_p, nv_p, nc_p,
            nk_s[None], nv_s[None], nc_s.reshape(1, bs, n_state, conv_c))
```

```python
import functools
import math

import jax
import jax.numpy as jnp
from jax import lax
from jax.experimental import pallas as pl
from jax.experimental.pallas import tpu as pltpu

F32 = jnp.float32
BF16 = jnp.bfloat16

HEAD_DIM = 128
ROPE_DIM = HEAD_DIM // 4
ROPE_THETA = 500000.0
CONV_K = 31
RMS_EPS = 1e-6
LN_EPS = 1e-5
PAST_LEN = 8192
DILATIONS = (1, 4, 16)
SPAN = 128
MAX_WINDOW = 2048
ATTN_TILE = MAX_WINDOW
ATTN_UNROLL = 16
CONV_HALO = 32
FFN_DOWN_CHUNK = 1024
NORM_ROWS = 64
FFN_TRANSFER_PARTS = 4
WEIGHT_CAST_ROWS = 512

LANES = 128
VMEM_LIMIT_BYTES = 56 * 1024 * 1024


def _params(*sem):
    return pltpu.CompilerParams(dimension_semantics=sem, vmem_limit_bytes=VMEM_LIMIT_BYTES)


def _tile(n, pref, mult=8):
    if n <= pref:
        return n
    for t in range(pref - pref % mult, 0, -mult):
        if n % t == 0:
            return t
    raise ValueError(f"no tile for {n}")


def _rms_to_bf16(x, g):
    ms = jnp.mean(x * x, axis=-1, keepdims=True)
    return (x * lax.rsqrt(ms + RMS_EPS) * g).astype(BF16)


def _dot(a, b):
    return jnp.dot(a, b, preferred_element_type=F32)


def _dot_nt(a, b):
    return lax.dot_general(a, b, (((1,), (1,)), ((), ())), preferred_element_type=F32)


def _norm_kernel(x_ref, g_ref, h_ref):
    h_ref[...] = _rms_to_bf16(x_ref[...], g_ref[...])


def _cast_weight(w_ref, w_bf):
    k = w_ref.shape[0]
    rc = _tile(k, WEIGHT_CAST_ROWS)
    for r0 in range(0, k, rc):
        w_bf[r0:r0 + rc, :] = w_ref[r0:r0 + rc, :].astype(BF16)


def _row_parts(tm):
    half = tm // 2
    return ((0, tm),) if half % 16 else ((0, half), (half, tm))


def _qkv_kernel(h_ref, w_ref, tab_ref, o_ref, w_bf):
    @pl.when(pl.program_id(1) == 0)
    def _():
        _cast_weight(w_ref, w_bf)

    for r0, r1 in _row_parts(h_ref.shape[0]):
        z = _dot(h_ref[r0:r1, :], w_bf[...])
        c, s1, s2 = tab_ref[0, 0, r0:r1, :], tab_ref[0, 1, r0:r1, :], tab_ref[0, 2, r0:r1, :]
        for j in range(z.shape[1] // HEAD_DIM):
            zj = z[:, j * HEAD_DIM:(j + 1) * HEAD_DIM]
            lo = pltpu.roll(zj, ROPE_DIM // 2, 1)
            hi = pltpu.roll(zj, HEAD_DIM - ROPE_DIM // 2, 1)
            o_ref[r0:r1, j * HEAD_DIM:(j + 1) * HEAD_DIM] = zj * c + lo * s1 + hi * s2


def _glu_kernel(h_ref, wa_ref, wg_ref, *rest, rider):
    if rider:
        hs_ref, u_ref, us_ref, wa_bf, wg_bf = rest
    else:
        u_ref, wa_bf, wg_bf = rest

    def glu(h):
        return _dot(h, wa_bf[...]) * jax.nn.sigmoid(_dot(h, wg_bf[...]))

    @pl.when(pl.program_id(1) == 0)
    def _():
        _cast_weight(wa_ref, wa_bf)
        _cast_weight(wg_ref, wg_bf)
        if rider:
            us_ref[...] = glu(hs_ref[...])

    for r0, r1 in _row_parts(h_ref.shape[0]):
        u_ref[r0:r1, :] = glu(h_ref[r0:r1, :])


def _rope_tables(pos):
    half = ROPE_DIM // 2
    inv = ROPE_THETA ** (-jnp.arange(half, dtype=F32) / half)
    ang = pos.astype(F32)[:, None] * inv[None, :]
    cos, sin = jnp.cos(ang), jnp.sin(ang)
    m = pos.shape[0]
    one = jnp.ones((m, HEAD_DIM - ROPE_DIM), F32)
    zero = jnp.zeros((m, HEAD_DIM - ROPE_DIM), F32)
    zh = jnp.zeros((m, half), F32)
    c = jnp.concatenate([cos, cos, one], axis=1)
    s1 = jnp.concatenate([zh, sin, zero], axis=1)
    s2 = jnp.concatenate([-sin, zh, zero], axis=1)
    rot = jnp.stack([c, s1, s2])
    ident = jnp.stack([jnp.ones_like(c), jnp.zeros_like(c), jnp.zeros_like(c)])
    return jnp.stack([rot, ident])


def _norm1(x, g1):
    m, d = x.shape
    tr = _tile(m, 256)
    return pl.pallas_call(
        _norm_kernel,
        grid=(m // tr,),
        in_specs=[pl.BlockSpec((tr, d), lambda i: (i, 0)), pl.BlockSpec((1, d), lambda i: (0, 0))],
        out_specs=pl.BlockSpec((tr, d), lambda i: (i, 0)),
        out_shape=jax.ShapeDtypeStruct((m, d), BF16),
        compiler_params=_params("parallel"),
        name="norm1",
    )(x, g1)


def _weight_mode(m, tm):
    return pl.Buffered(1) if m // tm > 1 else None


def _qkv_proj(h, w_in, pos, attn_w):
    m, d = h.shape
    tm = _tile(m, 1024, 16)
    tn = _tile(attn_w, 1024, LANES)
    once = _weight_mode(m, tm)
    n_rope = 2 * attn_w // tn
    return pl.pallas_call(
        _qkv_kernel,
        grid=(3 * attn_w // tn, m // tm),
        in_specs=[pl.BlockSpec((tm, d), lambda n, i: (i, 0)),
                  pl.BlockSpec((d, tn), lambda n, i: (0, n), pipeline_mode=once),
                  pl.BlockSpec((1, 3, tm, HEAD_DIM), lambda n, i: (jnp.where(n < n_rope, 0, 1), 0, i, 0))],
        out_specs=pl.BlockSpec((tm, tn), lambda n, i: (i, n)),
        out_shape=jax.ShapeDtypeStruct((m, 3 * attn_w), F32),
        scratch_shapes=[pltpu.VMEM((d, tn), BF16)],
        compiler_params=_params("arbitrary", "arbitrary"),
        name="qkv_proj",
    )(h, w_in, _rope_tables(pos))


def _glu_proj(h, w_in, attn_w, conv_c, rider=None):
    m, d = h.shape
    tm = _tile(m, 1024, 16)
    tc = _tile(conv_c, 512, LANES)
    once = _weight_mode(m, tm)
    a0 = 3 * attn_w // tc
    g0 = (3 * attn_w + conv_c) // tc
    in_specs = [pl.BlockSpec((tm, d), lambda n, i: (i, 0)),
                pl.BlockSpec((d, tc), lambda n, i: (0, a0 + n), pipeline_mode=once),
                pl.BlockSpec((d, tc), lambda n, i: (0, g0 + n), pipeline_mode=once)]
    out_specs = [pl.BlockSpec((tm, tc), lambda n, i: (i, n))]
    out_shape = [jax.ShapeDtypeStruct((m, conv_c), F32)]
    args = [h, w_in, w_in]
    if rider is not None:
        ms = rider.shape[0]
        in_specs.append(pl.BlockSpec((ms, d), lambda n, i: (0, 0)))
        out_specs.append(pl.BlockSpec((ms, tc), lambda n, i: (0, n)))
        out_shape.append(jax.ShapeDtypeStruct((ms, conv_c), F32))
        args.append(rider)
    return pl.pallas_call(
        functools.partial(_glu_kernel, rider=rider is not None),
        grid=(conv_c // tc, m // tm),
        in_specs=in_specs,
        out_specs=out_specs,
        out_shape=out_shape,
        scratch_shapes=[pltpu.VMEM((d, tc), BF16)] * 2,
        compiler_params=_params("arbitrary", "arbitrary"),
        name="glu_proj",
    )(*args)


def _rows(start, size, stride):
    return pl.ds(start, size) if stride == 1 else pl.ds(start, size, stride=stride)


def _attn_prompt_kernel(q_ref, kp_ref, kc_ref, vp_ref, vc_ref, *rest):
    n_w = (len(rest) - 4) // 2
    w_in, o_ref, w_out = rest[:n_w], rest[n_w], rest[n_w + 1:2 * n_w + 1]
    num_scr, m_scr, l_scr = rest[2 * n_w + 1:]
    for src, dst in zip(w_in, w_out):
        rc = _tile(src.shape[0], 16, 16)
        for r0 in range(0, src.shape[0], rc):
            dst[r0:r0 + rc, :] = src[r0:r0 + rc, :].astype(BF16)

    tile = pl.program_id(1)
    t = q_ref.shape[0]
    scale = HEAD_DIM ** -0.5
    qi = lax.broadcasted_iota(jnp.int32, (SPAN, 2 * SPAN), 0)
    kj = lax.broadcasted_iota(jnp.int32, (SPAN, 2 * SPAN), 1)
    band = (kj >= qi) & (kj <= qi + SPAN)
    first_key = jnp.where(tile > 0, 0, SPAN)
    band_first = band & (kj >= first_key)

    def attend(gi, d, q_start, q, k, v, mask):
        s = _dot_nt(q.astype(BF16), k.astype(BF16)) * scale
        s = jnp.where(mask, s, -jnp.inf)
        m = jnp.max(s, axis=-1, keepdims=True)
        p = jnp.exp(s - m)
        l = jnp.sum(p, axis=-1, keepdims=True)
        num = _dot(p.astype(BF16), v.astype(BF16))
        rows = _rows(q_start, SPAN, d)
        num_scr[gi, rows, :] = num
        m_scr[gi, rows, :] = jnp.broadcast_to(m, (SPAN, HEAD_DIM))
        l_scr[gi, rows, :] = jnp.broadcast_to(l, (SPAN, HEAD_DIM))

    def first_block(gi, d, r):
        q = q_ref[_rows(r, SPAN, d), :]
        k = jnp.concatenate([kp_ref[_rows(t - d * SPAN + r, SPAN, d), :],
                             kc_ref[_rows(r, SPAN, d), :]], axis=0)
        v = jnp.concatenate([vp_ref[_rows(t - d * SPAN + r, SPAN, d), :],
                             vc_ref[_rows(r, SPAN, d), :]], axis=0)
        attend(gi, d, r, q, k, v, band_first)

    def later_block(gi, d, r, b):
        q_start = r + d * SPAN * b
        if d == 1:
            q_start = pl.multiple_of(q_start, SPAN)
        k_rows = _rows(q_start - d * SPAN, 2 * SPAN, d)
        attend(gi, d, q_start, q_ref[_rows(q_start, SPAN, d), :], kc_ref[k_rows, :], vc_ref[k_rows, :], band)

    u = ATTN_UNROLL
    for gi, d in enumerate(DILATIONS):
        n_blocks = t // (d * SPAN)
        if n_blocks >= u:
            assert n_blocks % u == 0

            def class_body(r, carry, gi=gi, d=d, n_blocks=n_blocks):
                first_block(gi, d, r)
                for b in range(1, u):
                    later_block(gi, d, r, b)

                def rest(i, c):
                    for j in range(u):
                        later_block(gi, d, r, i * u + j)
                    return c

                if n_blocks > u:
                    lax.fori_loop(1, n_blocks // u, rest, 0)
                return carry

            if d == 1:
                class_body(0, 0)
            else:
                lax.fori_loop(0, d, class_body, 0)
        else:
            per_iter = u // n_blocks
            assert u % n_blocks == 0 and d % per_iter == 0

            def class_group(i, carry, gi=gi, d=d, n_blocks=n_blocks, per_iter=per_iter):
                for j in range(per_iter):
                    r = i * per_iter + j
                    first_block(gi, d, r)
                    for b in range(1, n_blocks):
                        later_block(gi, d, r, b)
                return carry

            lax.fori_loop(0, d // per_iter, class_group, 0)

    def combine(c, carry):
        rows = pl.ds(pl.multiple_of(c * SPAN, SPAN), SPAN)
        ms = [m_scr[g, rows, :] for g in range(len(DILATIONS))]
        m_all = functools.reduce(jnp.maximum, ms)
        num = jnp.zeros((SPAN, HEAD_DIM), F32)
        den = jnp.zeros((SPAN, HEAD_DIM), F32)
        for g in range(len(DILATIONS)):
            sc = jnp.exp(ms[g] - m_all)
            num = num + num_scr[g, rows, :] * sc
            den = den + l_scr[g, rows, :] * sc
        o_ref[rows, :] = (num / den).astype(o_ref.dtype)
        return carry

    lax.fori_loop(0, t // SPAN, combine, 0)


def _attn_prompt(qkv, n_heads, weights=()):
    s = qkv.shape[0]
    t = ATTN_TILE
    assert s % t == 0, "prompt length must be a multiple of the dilation tile"
    n_tiles = s // t
    cur = lambda off: pl.BlockSpec((t, HEAD_DIM), lambda h, i: (i, off + h))
    prev = lambda off: pl.BlockSpec((t, HEAD_DIM), lambda h, i: (jnp.maximum(i - 1, 0), off + h))
    w_specs, w_shapes = [], []
    for w in weights:
        rows, cols = w.shape
        rb = next(r for r in range(16, rows + 1, 16) if rows % r == 0 and rows // r <= n_heads * n_tiles)
        spec = pl.BlockSpec((rb, cols), lambda h, i, nb=rows // rb: (jnp.minimum(h * n_tiles + i, nb - 1), 0))
        w_specs.append(spec)
        w_shapes.append(jax.ShapeDtypeStruct(w.shape, BF16))
    out = pl.pallas_call(
        _attn_prompt_kernel,
        grid=(n_heads, n_tiles),
        in_specs=[cur(0), prev(n_heads), cur(n_heads), prev(2 * n_heads), cur(2 * n_heads)] + w_specs,
        out_specs=[pl.BlockSpec((t, HEAD_DIM), lambda h, i: (i, h))] + w_specs,
        out_shape=[jax.ShapeDtypeStruct((s, n_heads * HEAD_DIM), BF16)] + w_shapes,
        scratch_shapes=[pltpu.VMEM((len(DILATIONS), t, HEAD_DIM), F32)] * 3,
        compiler_params=_params("arbitrary", "arbitrary"),
        name="attn_prompt",
    )(qkv, qkv, qkv, qkv, qkv, *weights)
    return out[0], tuple(out[1:])


def _attn_sample_kernel(qkv_ref, kt_ref, vt_ref, kf_ref, vf_ref, o_ref, kn_scr, vn_scr, *, n_heads):
    n_new = qkv_ref.shape[1]
    tail = kt_ref.shape[1]
    d_far = DILATIONS[-1]
    kn_scr[0:tail] = kt_ref[0]
    vn_scr[0:tail] = vt_ref[0]
    kn_scr[tail:tail + n_new] = qkv_ref[0, :, n_heads:2 * n_heads, :]
    vn_scr[tail:tail + n_new] = qkv_ref[0, :, 2 * n_heads:3 * n_heads, :]
    n_near_far = tail // d_far + 1

    for i in range(n_new):
        q = qkv_ref[0, i, 0:n_heads, :] * (HEAD_DIM ** -0.5)
        lists = []
        for d in DILATIONS[:-1]:
            rows = _rows(tail + i - d * SPAN, SPAN + 1, d)
            lists.append((kn_scr[rows], vn_scr[rows]))
        rows = _rows(tail + i - d_far * (n_near_far - 1), n_near_far, d_far)
        lists.append((kn_scr[rows], vn_scr[rows]))
        lists.append((kf_ref[0, :, i], vf_ref[0, :, i]))
        scores = [jnp.sum(k * q[None], axis=-1, keepdims=True) for k, _ in lists]
        m = functools.reduce(jnp.maximum, [jnp.max(s, axis=0) for s in scores])
        den = jnp.zeros((n_heads, 1), F32)
        num = jnp.zeros((n_heads, HEAD_DIM), F32)
        for s, (_, v) in zip(scores, lists):
            p = jnp.exp(s - m[None])
            den = den + jnp.sum(p, axis=0)
            num = num + jnp.sum(p * v, axis=0)
        o_ref[0, i] = num / den


def _attn_sample(qkv, cache_k, cache_v):
    b, n_new, _, _ = qkv.shape
    buf, n_heads = cache_k.shape[1], cache_k.shape[2]
    d_mid, d_far = DILATIONS[1], DILATIONS[2]
    tail = d_mid * SPAN
    assert buf == d_far * SPAN and n_new <= d_mid and tail % d_far == 0
    n_far = (buf - tail) // d_far
    far_shape = (b, buf // d_far, d_far, n_heads, HEAD_DIM)
    near = pl.BlockSpec((1, tail, n_heads, HEAD_DIM), lambda i: (i, buf // tail - 1, 0, 0))
    far = pl.BlockSpec((1, n_far, n_new, n_heads, HEAD_DIM), lambda i: (i, 0, 0, 0, 0))
    new = pl.BlockSpec((1, n_new, 3 * n_heads, HEAD_DIM), lambda i: (i, 0, 0, 0))
    ctx = pltpu.VMEM((tail + 8, n_heads, HEAD_DIM), F32)
    return pl.pallas_call(
        functools.partial(_attn_sample_kernel, n_heads=n_heads),
        grid=(b,),
        in_specs=[new, near, near, far, far],
        out_specs=pl.BlockSpec((1, n_new, n_heads, HEAD_DIM), lambda i: (i, 0, 0, 0)),
        out_shape=jax.ShapeDtypeStruct((b, n_new, n_heads, HEAD_DIM), F32),
        scratch_shapes=[ctx, ctx],
        compiler_params=_params("parallel"),
        name="attn_sample",
    )(qkv, cache_k, cache_v, cache_k.reshape(far_shape), cache_v.reshape(far_shape))


def _ln_swish(y, g, b):
    mu = jnp.mean(y, axis=-1, keepdims=True)
    yc = y - mu
    var = jnp.mean(yc * yc, axis=-1, keepdims=True)
    z = yc * lax.rsqrt(var + LN_EPS) * g + b
    return z * jax.nn.sigmoid(z)


def _conv_prompt_kernel(up_ref, u_ref, w_ref, b_ref, lg_ref, lb_ref, o_ref, ctx_scr, y_scr):
    tm, c = u_ref.shape
    n_slabs = c // LANES
    slab = lambda cs: slice(cs * LANES, (cs + 1) * LANES)

    @pl.when(pl.program_id(0) > 0)
    def _():
        for cs in range(n_slabs):
            ctx_scr[cs, 0:CONV_HALO, :] = up_ref[:, slab(cs)]

    @pl.when(pl.program_id(0) == 0)
    def _():
        for cs in range(n_slabs):
            ctx_scr[cs, 0:CONV_HALO, :] = jnp.zeros((CONV_HALO, LANES), F32)

    for cs in range(n_slabs):
        ctx_scr[cs, CONV_HALO:CONV_HALO + tm, :] = u_ref[:, slab(cs)]

    off = CONV_HALO - (CONV_K - 1)
    rc = 32

    for cs in range(n_slabs):
        taps = [jnp.broadcast_to(w_ref[j:j + 1, slab(cs)], (rc, LANES)) for j in range(CONV_K)]
        bias = jnp.broadcast_to(b_ref[:, slab(cs)], (rc, LANES))

        def chunk(i, carry, cs=cs, taps=taps, bias=bias):
            r0 = i * (2 * rc)
            for phase in range(2):
                accs = [bias, jnp.zeros((rc, LANES), F32)]
                for j in range(CONV_K):
                    accs[j % 2] = accs[j % 2] + ctx_scr[cs, pl.ds(r0 + phase + off + j, rc, stride=2), :] * taps[j]
                y_scr[cs, pl.ds(r0 + phase, rc, stride=2), :] = accs[0] + accs[1]
            return carry

        lax.fori_loop(0, tm // (2 * rc), chunk, 0)

    ln_rows = 16
    ln_unroll = 4

    def ln_rows_at(row0):
        rows = pl.ds(pl.multiple_of(row0, ln_rows), ln_rows)
        ys = [y_scr[cs, rows, :] for cs in range(n_slabs)]
        mu = jnp.sum(functools.reduce(jnp.add, ys), axis=-1, keepdims=True) * (1.0 / c)
        ycs = [y - mu for y in ys]
        var = jnp.sum(functools.reduce(jnp.add, [yc * yc for yc in ycs]), axis=-1, keepdims=True) * (1.0 / c)
        inv = lax.rsqrt(var + LN_EPS)
        for cs in range(n_slabs):
            z = ycs[cs] * inv * lg_ref[:, slab(cs)] + lb_ref[:, slab(cs)]
            o_ref[rows, slab(cs)] = (z * jax.nn.sigmoid(z)).astype(o_ref.dtype)

    def ln_chunk(i, carry):
        for j in range(ln_unroll):
            ln_rows_at((i * ln_unroll + j) * ln_rows)
        return carry

    lax.fori_loop(0, tm // (ln_rows * ln_unroll), ln_chunk, 0)


def _conv_prompt(u, conv_w, conv_b, ln_g, ln_b):
    s, c = u.shape
    tm = _tile(s, 256, CONV_HALO)
    vec = pl.BlockSpec((1, c), lambda i: (0, 0))
    return pl.pallas_call(
        _conv_prompt_kernel,
        grid=(s // tm,),
        in_specs=[pl.BlockSpec((CONV_HALO, c), lambda i: (jnp.maximum(i * (tm // CONV_HALO) - 1, 0), 0)),
                  pl.BlockSpec((tm, c), lambda i: (i, 0)),
                  pl.BlockSpec((CONV_K, c), lambda i: (0, 0)), vec, vec, vec],
        out_specs=pl.BlockSpec((tm, c), lambda i: (i, 0)),
        out_shape=jax.ShapeDtypeStruct((s, c), BF16),
        scratch_shapes=[pltpu.VMEM((c // LANES, CONV_HALO + tm, LANES), F32),
                        pltpu.VMEM((c // LANES, tm, LANES), F32)],
        compiler_params=_params("parallel"),
        name="conv_prompt",
    )(u, u, conv_w, conv_b, ln_g, ln_b)


def _conv_sample_kernel(st_ref, u_ref, w_ref, b_ref, lg_ref, lb_ref, o_ref, ns_ref, ctx_scr):
    n_state, n_new = st_ref.shape[1], u_ref.shape[1]
    ctx_scr[0:n_state, :] = st_ref[0]
    ctx_scr[n_state:n_state + n_new, :] = u_ref[0]
    acc = jnp.zeros(u_ref.shape[1:], F32)
    for j in range(CONV_K):
        acc = acc + ctx_scr[j:j + n_new, :] * w_ref[j:j + 1, :]
    o_ref[0] = _ln_swish(acc + b_ref[...], lg_ref[...], lb_ref[...])
    ns_ref[0] = ctx_scr[n_new:n_new + n_state, :]


def _conv_sample(state, u, conv_w, conv_b, ln_g, ln_b):
    b, n_state, c = state.shape
    n_new = u.shape[1]
    assert n_state == CONV_K - 1
    vec = pl.BlockSpec((1, c), lambda i: (0, 0))
    return pl.pallas_call(
        _conv_sample_kernel,
        grid=(b,),
        in_specs=[pl.BlockSpec((1, n_state, c), lambda i: (i, 0, 0)),
                  pl.BlockSpec((1, n_new, c), lambda i: (i, 0, 0)),
                  pl.BlockSpec((CONV_K, c), lambda i: (0, 0)), vec, vec, vec],
        out_specs=[pl.BlockSpec((1, n_new, c), lambda i: (i, 0, 0)),
                   pl.BlockSpec((1, n_state, c), lambda i: (i, 0, 0))],
        out_shape=[jax.ShapeDtypeStruct((b, n_new, c), F32),
                   jax.ShapeDtypeStruct((b, n_state, c), F32)],
        scratch_shapes=[pltpu.VMEM((n_state + n_new, c), F32)],
        compiler_params=_params("parallel"),
        name="conv_sample",
    )(state, u, conv_w, conv_b, ln_g, ln_b)


def _outproj_kernel(x_ref, a_ref, c_ref, wa_ref, wc_ref, *rest, rider):
    if rider:
        xs_ref, as_ref, cs_ref, o_ref, os_ref, wa_bf, wc_bf = rest
    else:
        o_ref, wa_bf, wc_bf = rest

    def project(x, a, c):
        return x + (_dot(a.astype(BF16), wa_bf[...]) + _dot(c.astype(BF16), wc_bf[...]))

    @pl.when(pl.program_id(1) == 0)
    def _():
        _cast_weight(wa_ref, wa_bf)
        _cast_weight(wc_ref, wc_bf)
        if rider:
            os_ref[...] = project(xs_ref[...], as_ref[...], cs_ref[...])

    for r0, r1 in _row_parts(x_ref.shape[0]):
        o_ref[r0:r1, :] = project(x_ref[r0:r1, :], a_ref[r0:r1, :], c_ref[r0:r1, :])


def _outproj(x, attn, conv, w_out, rider=None):
    m, d = x.shape
    a, c = attn.shape[1], conv.shape[1]
    assert a == c
    tm = _tile(m, 1024, 16)
    tn = _tile(d, 512, LANES)
    in_specs = [pl.BlockSpec((tm, tn), lambda n, i: (i, n)),
                pl.BlockSpec((tm, a), lambda n, i: (i, 0)),
                pl.BlockSpec((tm, c), lambda n, i: (i, 0)),
                pl.BlockSpec((a, tn), lambda n, i: (0, n)),
                pl.BlockSpec((c, tn), lambda n, i: (1, n))]
    out_specs = [pl.BlockSpec((tm, tn), lambda n, i: (i, n))]
    out_shape = [jax.ShapeDtypeStruct((m, d), F32)]
    args = [x, attn, conv, w_out, w_out]
    if rider is not None:
        xs, attn_s, conv_s = rider
        ms = xs.shape[0]
        in_specs += [pl.BlockSpec((ms, tn), lambda n, i: (0, n)),
                     pl.BlockSpec((ms, a), lambda n, i: (0, 0)),
                     pl.BlockSpec((ms, c), lambda n, i: (0, 0))]
        out_specs.append(pl.BlockSpec((ms, tn), lambda n, i: (0, n)))
        out_shape.append(jax.ShapeDtypeStruct((ms, d), F32))
        args += [xs, attn_s, conv_s]
    return pl.pallas_call(
        functools.partial(_outproj_kernel, rider=rider is not None),
        grid=(d // tn, m // tm),
        in_specs=in_specs,
        out_specs=out_specs,
        out_shape=out_shape,
        scratch_shapes=[pltpu.VMEM((a, tn), BF16), pltpu.VMEM((c, tn), BF16)],
        compiler_params=_params("arbitrary", "arbitrary"),
        name="out_proj",
    )(*args)


def _ffn_kernel(*refs, roll, rider):
    refs = list(refs)
    take = lambda n: [refs.pop(0) for _ in range(n)]
    x_hbm, g2_ref, wg_ref, wu_ref, wd_ref, gf_ref = take(6)
    xs_hbm, = take(1) if rider else (None,)
    caches = take(3) if roll else None
    y_hbm, = take(1)
    ys_hbm, = take(1) if rider else (None,)
    rolled = take(2) if roll else None
    acc_ref, h_ref, x_sem, y_sem = take(4)
    acc_s, h_s, s_sem = take(3) if rider else (None, None, None)
    i, f = pl.program_id(0), pl.program_id(1)
    last_f = pl.num_programs(1) - 1
    if roll:
        slots, in_sem, out_sem, new_sem = take(4)
        _cache_roll_step(i * pl.num_programs(1) + f, roll, caches[:2], caches[2], rolled,
                         slots, in_sem, out_sem, new_sem)
    tm, d = acc_ref.shape
    parts = FFN_TRANSFER_PARTS if tm % (16 * FFN_TRANSFER_PARTS) == 0 else 1
    tp = tm // parts

    def x_copy(t, c):
        return pltpu.make_async_copy(x_hbm.at[pl.ds(pl.multiple_of(t * tm + c * tp, tp), tp)],
                                     acc_ref.at[pl.ds(c * tp, tp)], x_sem.at[c])

    def y_copy(t, c):
        return pltpu.make_async_copy(acc_ref.at[pl.ds(c * tp, tp)],
                                     y_hbm.at[pl.ds(pl.multiple_of(t * tm + c * tp, tp), tp)], y_sem.at[c])

    def by_row_chunks(start, size, body):
        rc = _tile(size, NORM_ROWS)

        def step(c, carry):
            body(pl.ds(pl.multiple_of(start + c * rc, rc), rc))
            return carry
        lax.fori_loop(0, size // rc, step, 0)

    def norm_in(acc, h):
        def body(rows):
            h[rows, :] = _rms_to_bf16(acc[rows, :], g2_ref[...])
        return body

    def norm_out(acc):
        def body(rows):
            y = acc[rows, :]
            ms = jnp.mean(y * y, axis=-1, keepdims=True)
            acc[rows, :] = y * lax.rsqrt(ms + RMS_EPS) * gf_ref[...]
        return body

    def swiglu_into(acc, h_rows):
        h = h_rows[...]
        gate = _dot(h, wg_ref[...])
        act = (gate * jax.nn.sigmoid(gate) * _dot(h, wu_ref[...])).astype(BF16)
        dc = _tile(d, FFN_DOWN_CHUNK, LANES)
        for c0 in range(0, d, dc):
            acc[:, c0:c0 + dc] += _dot(act, wd_ref[:, c0:c0 + dc])

    @pl.when(f == 0)
    def _():
        for c in range(parts):
            @pl.when(i > 0)
            def _(c=c):
                y_copy(i - 1, c).wait()
            x_copy(i, c).start()
        for c in range(parts):
            x_copy(i, c).wait()
            by_row_chunks(c * tp, tp, norm_in(acc_ref, h_ref))
        if rider:
            @pl.when(i == 0)
            def _():
                cp = pltpu.make_async_copy(xs_hbm, acc_s, s_sem.at[0])
                cp.start()
                cp.wait()
                by_row_chunks(0, acc_s.shape[0], norm_in(acc_s, h_s))

    swiglu_into(acc_ref, h_ref)
    if rider:
        @pl.when(i == 0)
        def _():
            swiglu_into(acc_s, h_s)

    @pl.when(f == last_f)
    def _():
        for c in range(parts):
            by_row_chunks(c * tp, tp, norm_out(acc_ref))
            y_copy(i, c).start()

        @pl.when(i == pl.num_programs(0) - 1)
        def _():
            for c in range(parts):
                y_copy(i, c).wait()

        if rider:
            @pl.when(i == 0)
            def _():
                by_row_chunks(0, acc_s.shape[0], norm_out(acc_s))
                cp = pltpu.make_async_copy(acc_s, ys_hbm, s_sem.at[1])
                cp.start()
                cp.wait()


def _cache_roll_step(step, roll, caches, new_ref, rolled, slots, in_sem, out_sem, new_sem):
    n_chunks, per_seq, n_heads = roll
    rows = slots.shape[2]
    n_new = new_ref.shape[1]

    def chunk_copies(c):
        seq, j, slot = c // per_seq, c % per_seq, c % 2
        pairs = []
        for i in range(2):
            buf = slots.at[slot, i]
            pairs.append((
                pltpu.make_async_copy(caches[i].at[seq, pl.ds(j * rows + n_new, rows)], buf, in_sem.at[slot, i]),
                pltpu.make_async_copy(buf, rolled[i].at[seq, pl.ds(j * rows, rows)], out_sem.at[slot, i])))
        return pairs

    def new_copies(c):
        seq = c // per_seq
        return [pltpu.make_async_copy(new_ref.at[seq, :, pl.ds((i + 1) * n_heads, n_heads)],
                                      rolled[i].at[seq, pl.ds(per_seq * rows, n_new)], new_sem.at[i])
                for i in range(2)]

    @pl.when((step >= 2) & (step < n_chunks + 2))
    def _():
        for _, out in chunk_copies(step - 2):
            out.wait()

    @pl.when(step < n_chunks)
    def _():
        for inp, _ in chunk_copies(step):
            inp.start()

    @pl.when((step >= 1) & (step < n_chunks + 1))
    def _():
        for inp, out in chunk_copies(step - 1):
            inp.wait()
            out.start()

    @pl.when((step >= 1) & (step < n_chunks + 1) & ((step - 1) % per_seq == 0))
    def _():
        for cp in new_copies(step - 1):
            cp.wait()

    @pl.when((step < n_chunks) & (step % per_seq == 0))
    def _():
        for cp in new_copies(step):
            cp.start()


def _ffn(x, g2, w_gate, w_up, w_down, gf, rider=None, caches=None):
    m, d = x.shape
    dff = w_gate.shape[1]
    tm = _tile(m, 1024, 16)
    tf = _tile(dff, 256, LANES)
    nf = dff // tf
    grid = (m // tm, nf)
    vec = pl.BlockSpec((1, d), lambda i, f: (0, 0))
    any_spec = pl.BlockSpec(memory_space=pl.ANY)
    in_specs = [any_spec, vec,
                pl.BlockSpec((d, tf), lambda i, f: (0, f)),
                pl.BlockSpec((d, tf), lambda i, f: (0, f)),
                pl.BlockSpec((tf, d), lambda i, f: (f, 0)), vec]
    out_specs = [any_spec]
    out_shape = [jax.ShapeDtypeStruct((m, d), F32)]
    args = [x, g2, w_gate, w_up, w_down, gf]
    scratch = [pltpu.VMEM((tm, d), F32), pltpu.VMEM((tm, d), BF16),
               pltpu.SemaphoreType.DMA((FFN_TRANSFER_PARTS,)), pltpu.SemaphoreType.DMA((FFN_TRANSFER_PARTS,))]
    if rider is not None:
        in_specs.append(any_spec)
        out_specs.append(any_spec)
        out_shape.append(jax.ShapeDtypeStruct(rider.shape, F32))
        args.append(rider)
        scratch += [pltpu.VMEM(rider.shape, F32), pltpu.VMEM(rider.shape, BF16), pltpu.SemaphoreType.DMA((2,))]
    roll = None
    if caches is not None:
        cache_k, cache_v, qkv_new = caches
        b, buf, n_heads, _ = cache_k.shape
        n_new = qkv_new.shape[1]
        assert buf == MAX_WINDOW, "the rolled cache keeps exactly the window"
        old = buf - n_new
        steps = grid[0] * grid[1]
        per_seq = max(p for p in range(2, old + 1) if old % p == 0 and b * p + 2 <= steps)
        n_chunks = b * per_seq
        in_specs += [any_spec] * 3
        out_specs += [any_spec] * 2
        out_shape += [jax.ShapeDtypeStruct((b, buf, n_heads, HEAD_DIM), F32)] * 2
        args += [cache_k, cache_v, qkv_new]
        scratch += [pltpu.VMEM((2, 2, old // per_seq, n_heads, HEAD_DIM), F32),
                    pltpu.SemaphoreType.DMA((2, 2)), pltpu.SemaphoreType.DMA((2, 2)),
                    pltpu.SemaphoreType.DMA((2,))]
        roll = (n_chunks, per_seq, n_heads)
    return pl.pallas_call(
        functools.partial(_ffn_kernel, roll=roll, rider=rider is not None),
        grid=grid,
        in_specs=in_specs,
        out_specs=out_specs,
        out_shape=out_shape,
        scratch_shapes=scratch,
        compiler_params=_params("arbitrary", "arbitrary"),
        name="ffn",
    )(*args)


def kernel(x_prompt, x_sample, cache_k, cache_v, state_conv, norm1_g, w_in, conv_w, conv_b, conv_ln_g,
           conv_ln_b, w_out, norm2_g, w_gate, w_up, w_down, final_g):
    depth = w_in.shape[0]
    bp, s, d = x_prompt.shape
    bs, t_new, _ = x_sample.shape
    assert depth == 1 and bp == 1, "one layer, one prompt sequence"
    attn_w = d // 2
    conv_c = d - attn_w
    n_heads = attn_w // HEAD_DIM
    buf = cache_k.shape[2]

    row = lambda v: v.reshape(1, -1)
    l = 0
    w_in_b, w_out_b = w_in[l], w_out[l]
    g1, g2, gf = row(norm1_g[l]), row(norm2_g[l]), row(final_g)
    cb, lg, lb = row(conv_b[l]), row(conv_ln_g[l]), row(conv_ln_b[l])

    ms = bs * t_new
    xs = x_sample.reshape(ms, d)
    pos_s = PAST_LEN + jnp.tile(jnp.arange(t_new, dtype=jnp.int32), bs)
    h_s = _norm1(xs, g1)
    qkv_s = _qkv_proj(h_s, w_in_b, pos_s, attn_w).reshape(bs, t_new, 3 * n_heads, HEAD_DIM)
    attn_s = _attn_sample(qkv_s, cache_k[l], cache_v[l])

    xp = x_prompt.reshape(s, d)
    h_p = _norm1(xp, g1)
    qkv_p = _qkv_proj(h_p, w_in_b, jnp.arange(s, dtype=jnp.int32), attn_w)
    u_p, u_s = _glu_proj(h_p, w_in_b, attn_w, conv_c, rider=h_s)
    conv_s, nc_s = _conv_sample(state_conv[l], u_s.reshape(bs, t_new, conv_c), conv_w[l], cb, lg, lb)
    attn_p, (w_gate_b, w_up_b, w_down_b) = _attn_prompt(qkv_p, n_heads, (w_gate[l], w_up[l], w_down[l]))
    conv_p = _conv_prompt(u_p, conv_w[l], cb, lg, lb)
    x1_p, x1_s = _outproj(xp, attn_p, conv_p, w_out_b,
                          rider=(xs, attn_s.reshape(ms, attn_w), conv_s.reshape(ms, conv_c)))
    y_p, y_s, nk_s, nv_s = _ffn(x1_p, g2, w_gate_b, w_up_b, w_down_b, gf, rider=x1_s,
                                caches=(cache_k[l], cache_v[l], qkv_s))
    keep_p = min(MAX_WINDOW, s)
    nk_p = qkv_p[s - keep_p:, attn_w:2 * attn_w].reshape(1, 1, keep_p, n_heads, HEAD_DIM)
    nv_p = qkv_p[s - keep_p:, 2 * attn_w:].reshape(1, 1, keep_p, n_heads, HEAD_DIM)
    n_state = CONV_K - 1
    assert s >= n_state
    nc_p = u_p[s - n_state:].reshape(1, 1, n_state, conv_c)

    return (y_p.reshape(1, s, d), y_s.reshape(bs, t_new, d), nk_p, nv_p, nc_p,
            nk_s[None], nv_s[None], nc_s.reshape(1, bs, n_state, conv_c))
```
